```python
import jax, jax.numpy as jnp
from jax import lax
import numpy as np

D_MODEL = 1024
BATCH = 8
SEQ = 2048
DEPTH = 1
DEC_BATCH = 128
DEC_SEQ = 1
PAST_LEN = 16384
PAGE_SIZE = 128

D_CONV = D_MODEL
CONV_WIDTH = 31
D_POOL = D_MODEL
POOL_WINDOWS = (2, 4, 8, 16)
POOL_GROUPS = len(POOL_WINDOWS)
POOL_GW = D_POOL // POOL_GROUPS
POOL_MAX = 16
D_FF = ((8 * D_MODEL // 3 + 127) // 128) * 128
N_MOD = 9
D_IN = 2 * D_CONV + D_POOL + 2 * D_MODEL
DN_ALPHA = (2.0 * DEPTH) ** 0.25
DN_BETA = (8.0 * DEPTH) ** -0.25
FFN_RES = 0.5
LN_EPS = 1e-5

kernel_name = "gated_conv_pool_macaron_decoder_step"


def layer_norm(x, g, b):
    xf = x.astype(jnp.float32)
    mu = jnp.mean(xf, axis=-1, keepdims=True)
    var = jnp.mean(jnp.square(xf - mu), axis=-1, keepdims=True)
    y = (xf - mu) * lax.rsqrt(var + LN_EPS)
    return (y * g.astype(jnp.float32) + b.astype(jnp.float32)).astype(x.dtype)


def swiglu(h, w_in, w_out):
    gt, up = jnp.split(h @ w_in, 2, axis=-1)
    return (jax.nn.silu(gt) * up) @ w_out


def causal_depthwise_conv(u, prev, w, b):
    ext = jnp.concatenate([prev.astype(u.dtype), u], axis=1)
    y = lax.conv_general_dilated(ext, w[:, None, :].astype(u.dtype), window_strides=(1,),
                                 padding='VALID', dimension_numbers=('NWC', 'WIO', 'NWC'),
                                 feature_group_count=u.shape[-1])
    return y + b, ext[:, -(CONV_WIDTH - 1):]


def multiscale_pool(u, prev, pos0):
    T = u.shape[1]
    P = POOL_MAX - 1
    ext = jnp.concatenate([prev.astype(u.dtype), u], axis=1)
    cs = jnp.cumsum(ext.astype(jnp.float32), axis=1)
    cs = jnp.pad(cs, ((0, 0), (1, 0), (0, 0)))
    pos = pos0 + jnp.arange(T, dtype=jnp.int32)
    means = []
    for g, w in enumerate(POOL_WINDOWS):
        sl = slice(g * POOL_GW, (g + 1) * POOL_GW)
        s = cs[:, P + 1:P + 1 + T, sl] - cs[:, P + 1 - w:P + 1 - w + T, sl]
        cnt = jnp.minimum(w, pos + 1).astype(jnp.float32)[None, :, None]
        means.append(s / cnt)
    mean = jnp.concatenate(means, axis=-1).astype(u.dtype)
    return mean - u, ext[:, -P:]


def hybrid_mixer(h, conv_prev, pool_prev, pos0, w_in, conv_w, conv_b, conv_ln_g, conv_ln_b,
                 w_conv_out, pool_w, pool_scale, w_pool_out, w_out):
    Bn, T = h.shape[0], h.shape[1]
    proj = h @ w_in
    a, bg, u_pool, gate_a, gate_b = jnp.split(
        proj, [D_CONV, 2 * D_CONV, 2 * D_CONV + D_POOL, 2 * D_CONV + D_POOL + D_MODEL], axis=-1)
    glu = a * jax.nn.sigmoid(bg)
    conv, new_conv = causal_depthwise_conv(glu, conv_prev, conv_w, conv_b)
    y_a = jax.nn.silu(layer_norm(conv, conv_ln_g, conv_ln_b)) @ w_conv_out
    pooled, new_pool = multiscale_pool(u_pool, pool_prev, pos0)
    mixed = jnp.einsum('btgi,gij->btgj', pooled.reshape(Bn, T, POOL_GROUPS, POOL_GW),
                       pool_w).reshape(Bn, T, D_POOL)
    y_b = (mixed * pool_scale) @ w_pool_out
    merged = jax.nn.sigmoid(gate_a) * y_a + jax.nn.sigmoid(gate_b) * y_b
    return merged @ w_out, new_conv, new_pool


def decoder_layer(x, c, conv_prev, pool_prev, pos0, w_ada, b_ada,
                  ffn1_w_in, ffn1_w_out, ln1_g, ln1_b,
                  w_in, conv_w, conv_b, conv_ln_g, conv_ln_b, w_conv_out,
                  pool_w, pool_scale, w_pool_out, w_out, ln2_g, ln2_b,
                  ffn2_w_in, ffn2_w_out, ln3_g, ln3_b):
    mod = (jax.nn.silu(c) @ w_ada + b_ada)[:, None, :]
    sh1, sc1, gt1, sh2, sc2, gt2, sh3, sc3, gt3 = jnp.split(mod, N_MOD, axis=-1)
    h = x * (1 + sc1) + sh1
    x = layer_norm(DN_ALPHA * x + FFN_RES * gt1 * swiglu(h, ffn1_w_in, ffn1_w_out), ln1_g, ln1_b)
    h = x * (1 + sc2) + sh2
    m, new_conv, new_pool = hybrid_mixer(h, conv_prev, pool_prev, pos0, w_in, conv_w, conv_b,
                                         conv_ln_g, conv_ln_b, w_conv_out, pool_w, pool_scale,
                                         w_pool_out, w_out)
    x = layer_norm(DN_ALPHA * x + gt2 * m, ln2_g, ln2_b)
    h = x * (1 + sc3) + sh3
    x = layer_norm(DN_ALPHA * x + FFN_RES * gt3 * swiglu(h, ffn2_w_in, ffn2_w_out), ln3_g, ln3_b)
    return x, new_conv, new_pool


def setup_inputs(seed: int = 0) -> dict:
    key = jax.random.key(seed)
    ks = jax.random.split(key, 32)
    f32 = jnp.float32

    def nrm(k, shape, scale):
        return jax.random.normal(k, shape, f32) * scale

    def gain(k, shape):
        return 1.0 + 0.05 * jax.random.normal(k, shape, f32)

    L = DEPTH
    return {
        "x_prompt": nrm(ks[0], (BATCH, SEQ, D_MODEL), 1.0),
        "x_sample": nrm(ks[1], (DEC_BATCH, DEC_SEQ, D_MODEL), 1.0),
        "state_conv": nrm(ks[2], (L, DEC_BATCH, CONV_WIDTH - 1, D_CONV), 0.5),
        "state_pool": nrm(ks[3], (L, DEC_BATCH, POOL_MAX - 1, D_POOL), 1.0),
        "c_prompt": nrm(ks[4], (BATCH, D_MODEL), 1.0),
        "c_sample": nrm(ks[5], (DEC_BATCH, D_MODEL), 1.0),
        "w_ada": nrm(ks[6], (L, D_MODEL, N_MOD * D_MODEL), 0.5 * D_MODEL ** -0.5),
        "b_ada": nrm(ks[7], (L, N_MOD * D_MODEL), 0.01),
        "ffn1_w_in": nrm(ks[8], (L, D_MODEL, 2 * D_FF), D_MODEL ** -0.5),
        "ffn1_w_out": nrm(ks[9], (L, D_FF, D_MODEL), DN_BETA * D_FF ** -0.5),
        "ln1_g": gain(ks[10], (L, D_MODEL)),
        "ln1_b": nrm(ks[11], (L, D_MODEL), 0.02),
        "w_in": nrm(ks[12], (L, D_MODEL, D_IN), D_MODEL ** -0.5),
        "conv_w": nrm(ks[13], (L, CONV_WIDTH, D_CONV), CONV_WIDTH ** -0.5),
        "conv_b": nrm(ks[14], (L, D_CONV), 0.02),
        "conv_ln_g": gain(ks[15], (L, D_CONV)),
        "conv_ln_b": nrm(ks[16], (L, D_CONV), 0.02),
        "w_conv_out": nrm(ks[17], (L, D_CONV, D_MODEL), D_CONV ** -0.5),
        "pool_w": nrm(ks[18], (L, POOL_GROUPS, POOL_GW, POOL_GW), POOL_GW ** -0.5),
        "pool_scale": 1.0 + 0.1 * jax.random.normal(ks[19], (L, D_POOL), f32),
        "w_pool_out": nrm(ks[20], (L, D_POOL, D_MODEL), D_POOL ** -0.5),
        "w_out": nrm(ks[21], (L, D_MODEL, D_MODEL), DN_BETA * D_MODEL ** -0.5),
        "ln2_g": gain(ks[22], (L, D_MODEL)),
        "ln2_b": nrm(ks[23], (L, D_MODEL), 0.02),
        "ffn2_w_in": nrm(ks[24], (L, D_MODEL, 2 * D_FF), D_MODEL ** -0.5),
        "ffn2_w_out": nrm(ks[25], (L, D_FF, D_MODEL), DN_BETA * D_FF ** -0.5),
        "ln3_g": gain(ks[26], (L, D_MODEL)),
        "ln3_b": nrm(ks[27], (L, D_MODEL), 0.02),
    }


def reference(x_prompt, x_sample, state_conv, state_pool, c_prompt, c_sample,
              w_ada, b_ada, ffn1_w_in, ffn1_w_out, ln1_g, ln1_b,
              w_in, conv_w, conv_b, conv_ln_g, conv_ln_b, w_conv_out,
              pool_w, pool_scale, w_pool_out, w_out, ln2_g, ln2_b,
              ffn2_w_in, ffn2_w_out, ln3_g, ln3_b):
    xp, xs = x_prompt, x_sample
    Bp = x_prompt.shape[0]
    conv_p_list, pool_p_list, conv_s_list, pool_s_list = [], [], [], []
    for l in range(DEPTH):
        lw = (w_ada[l], b_ada[l], ffn1_w_in[l], ffn1_w_out[l], ln1_g[l], ln1_b[l],
              w_in[l], conv_w[l], conv_b[l], conv_ln_g[l], conv_ln_b[l], w_conv_out[l],
              pool_w[l], pool_scale[l], w_pool_out[l], w_out[l], ln2_g[l], ln2_b[l],
              ffn2_w_in[l], ffn2_w_out[l], ln3_g[l], ln3_b[l])
        zc = jnp.zeros((Bp, CONV_WIDTH - 1, D_CONV), xp.dtype)
        zp = jnp.zeros((Bp, POOL_MAX - 1, D_POOL), xp.dtype)
        xp, nc_p, np_p = decoder_layer(xp, c_prompt, zc, zp, 0, *lw)
        xs, nc_s, np_s = decoder_layer(xs, c_sample, state_conv[l], state_pool[l], PAST_LEN, *lw)
        conv_p_list.append(nc_p)
        pool_p_list.append(np_p)
        conv_s_list.append(nc_s)
        pool_s_list.append(np_s)
    new_conv_prompt = jnp.stack(conv_p_list, axis=0)
    new_pool_prompt = jnp.stack(pool_p_list, axis=0)
    new_conv_sample = jnp.stack(conv_s_list, axis=0)
    new_pool_sample = jnp.stack(pool_s_list, axis=0)
    return (xp, xs, new_conv_prompt, new_pool_prompt, new_conv_sample, new_pool_sample)
```

```python
import functools

import jax
import jax.numpy as jnp
from jax import lax
from jax.experimental import pallas as pl
from jax.experimental.pallas import tpu as pltpu

D_MODEL = 1024
CONV_WIDTH = 31
CONV_HIST = CONV_WIDTH - 1
POOL_WINDOWS = (2, 4, 8, 16)
POOL_GW = D_MODEL // len(POOL_WINDOWS)
POOL_MAX = 16
POOL_HIST = POOL_MAX - 1
D_FF = ((8 * D_MODEL // 3 + 127) // 128) * 128
N_MOD = 9
DEPTH = 1
DN_ALPHA = (2.0 * DEPTH) ** 0.25
FFN_RES = 0.5
LN_EPS = 1e-5
PAST_LEN = 16384

LANES = 128
N_LANE_TILES = D_MODEL // LANES
FF_CHUNK = 256
TOKEN_TILE = 512
CONV_PAD = 32
POOL_PAD = 16
CONV_ROWS = 64
STATE_BATCH = 16
VMEM_LIMIT = 56 * 1024 * 1024

BF16 = jnp.bfloat16
F32 = jnp.float32


def _const_spec(shape):
    n = len(shape)
    return pl.BlockSpec(shape, lambda *_: (0,) * n, pipeline_mode=pl.Buffered(1))


def _dot(a, b):
    return jnp.dot(a, b, preferred_element_type=F32)


def _layer_norm(z, g, b):
    mu = jnp.mean(z, axis=-1, keepdims=True)
    zc = z - mu
    var = jnp.mean(zc * zc, axis=-1, keepdims=True)
    return zc * lax.rsqrt(var + LN_EPS) * g + b


def _mod_rows(ref, per_row, b):
    if per_row:
        return ref[...]
    return ref[pl.ds(b, 1), :]


def _ada_kernel(c_ref, w_ref, b_ref, o_ref):
    c = c_ref[...]
    s = (c * jax.nn.sigmoid(c)).astype(BF16)
    o_ref[...] = _dot(s, w_ref[...].astype(BF16)) + b_ref[...]


def _ada(c_all, w_ada, b_ada):
    rows = c_all.shape[0]
    return pl.pallas_call(
        _ada_kernel,
        grid=(N_MOD,),
        in_specs=[
            pl.BlockSpec((rows, D_MODEL), lambda i: (0, 0)),
            pl.BlockSpec((D_MODEL, D_MODEL), lambda i: (0, i)),
            pl.BlockSpec((1, D_MODEL), lambda i: (0, i)),
        ],
        out_specs=pl.BlockSpec((None, rows, D_MODEL), lambda i: (i, 0, 0)),
        out_shape=jax.ShapeDtypeStruct((N_MOD, rows, D_MODEL), F32),
        compiler_params=pltpu.CompilerParams(dimension_semantics=("arbitrary",)),
        name="ada",
    )(c_all, w_ada, b_ada.reshape(1, N_MOD * D_MODEL))


def _ffn_kernel(per_row, x_ref, sc_ref, sh_ref, gt_ref, w_in_ref, w_out_ref, g_ref, b_ref, o_ref):
    b = pl.program_id(0)
    x = x_ref[...]
    h = (x * (1.0 + _mod_rows(sc_ref, per_row, b)) + _mod_rows(sh_ref, per_row, b)).astype(BF16)
    acc = jnp.zeros(x.shape, F32)
    for c0 in range(0, D_FF, FF_CHUNK):
        gate = _dot(h, w_in_ref[:, c0:c0 + FF_CHUNK])
        up = _dot(h, w_in_ref[:, D_FF + c0:D_FF + c0 + FF_CHUNK])
        act = (gate * jax.nn.sigmoid(gate) * up).astype(BF16)
        acc = acc + _dot(act, w_out_ref[c0:c0 + FF_CHUNK, :])
    z = DN_ALPHA * x + FFN_RES * _mod_rows(gt_ref, per_row, b) * acc
    o_ref[...] = _layer_norm(z, g_ref[...], b_ref[...])


def _mod_spec(per_row, rows, idx, prompt_block):
    if per_row:
        return pl.BlockSpec((None, rows, D_MODEL), lambda b, t: (idx, 0, 0))
    return pl.BlockSpec((None, rows, D_MODEL), lambda b, t: (idx, prompt_block, 0))


def _ffn(x, mod, mod_idx, w_in, w_out, ln_g, ln_b, *, per_row, tile):
    nb, nt, _ = x.shape
    mod_rows = tile if per_row else nb
    prompt_block = None if per_row else (mod.shape[1] - nb) // nb
    specs = [_mod_spec(per_row, mod_rows, i, prompt_block) for i in mod_idx]
    return pl.pallas_call(
        functools.partial(_ffn_kernel, per_row),
        grid=(nb, nt // tile),
        in_specs=[pl.BlockSpec((None, tile, D_MODEL), lambda b, t: (b, t, 0))] + specs + [
            _const_spec((D_MODEL, 2 * D_FF)),
            _const_spec((D_FF, D_MODEL)),
            _const_spec((1, D_MODEL)),
            _const_spec((1, D_MODEL)),
        ],
        out_specs=pl.BlockSpec((None, tile, D_MODEL), lambda b, t: (b, t, 0)),
        out_shape=jax.ShapeDtypeStruct(x.shape, F32),
        compiler_params=pltpu.CompilerParams(
            dimension_semantics=("arbitrary", "arbitrary"), vmem_limit_bytes=VMEM_LIMIT),
        name="ffn_rows" if per_row else "ffn",
    )(x, mod, mod, mod, w_in, w_out, ln_g, ln_b)


def _mixer_tail(x, gate_rows, conv, pooled, ga, gb, cg_ref, cb_ref, w_co_ref, pw_ref, ps_ref,
                w_po_ref, w_o_ref, g_ref, b_ref):
    ya_in = _layer_norm(conv, cg_ref[...], cb_ref[...])
    ya_in = (ya_in * jax.nn.sigmoid(ya_in)).astype(BF16)
    merged = jax.nn.sigmoid(ga) * _dot(ya_in, w_co_ref[...])
    pooled = pooled.astype(BF16)
    mixed = jnp.concatenate(
        [_dot(pooled[:, g * POOL_GW:(g + 1) * POOL_GW], pw_ref[g]) for g in range(len(POOL_WINDOWS))],
        axis=-1)
    yb = _dot((mixed * ps_ref[...]).astype(BF16), w_po_ref[...])
    merged = (merged + jax.nn.sigmoid(gb) * yb).astype(BF16)
    z = DN_ALPHA * x + gate_rows * _dot(merged, w_o_ref[...])
    return _layer_norm(z, g_ref[...], b_ref[...])


def _mixer_kernel(x_ref, sc_ref, sh_ref, gt_ref, w_in_ref, cw_ref, cbias_ref, cg_ref, cb_ref,
                  w_co_ref, pw_ref, ps_ref, w_po_ref, w_o_ref, g_ref, b_ref,
                  o_ref, nconv_ref, npool_ref, conv_hist, conv_out, pool_hist):
    b = pl.program_id(0)
    t = pl.program_id(1)
    tile = x_ref.shape[0]

    @pl.when(t == 0)
    def _():
        conv_hist[:, 0:CONV_PAD, :] = jnp.zeros((N_LANE_TILES, CONV_PAD, LANES), F32)
        pool_hist[0:POOL_PAD, :] = jnp.zeros((POOL_PAD, D_MODEL), F32)

    @pl.when(t > 0)
    def _():
        conv_hist[:, 0:CONV_PAD, :] = conv_hist[:, tile:tile + CONV_PAD, :]
        pool_hist[0:POOL_PAD, :] = pool_hist[tile:tile + POOL_PAD, :]

    x = x_ref[...]
    h = (x * (1.0 + sc_ref[pl.ds(b, 1), :]) + sh_ref[pl.ds(b, 1), :]).astype(BF16)

    glu = _dot(h, w_in_ref[:, 0:D_MODEL])
    glu = glu * jax.nn.sigmoid(_dot(h, w_in_ref[:, D_MODEL:2 * D_MODEL]))
    for j in range(N_LANE_TILES):
        conv_hist[j, CONV_PAD:CONV_PAD + tile, :] = glu[:, j * LANES:(j + 1) * LANES]

    def conv_rows(r, carry):
        r0 = pl.multiple_of(r * CONV_ROWS, CONV_ROWS)
        for j in range(N_LANE_TILES):
            lanes = slice(j * LANES, (j + 1) * LANES)
            acc = jnp.broadcast_to(cbias_ref[:, lanes], (CONV_ROWS, LANES))
            for k in range(CONV_WIDTH):
                rows = conv_hist[j, pl.ds(r0 + (CONV_PAD - CONV_HIST + k), CONV_ROWS), :]
                acc = acc + rows * cw_ref[k:k + 1, lanes]
            conv_out[pl.ds(r0, CONV_ROWS), lanes] = acc
        return carry

    lax.fori_loop(0, tile // CONV_ROWS, conv_rows, 0)

    u = _dot(h, w_in_ref[:, 2 * D_MODEL:3 * D_MODEL])
    pool_hist[POOL_PAD:POOL_PAD + tile, :] = u
    pos1 = t * tile + 1 + lax.broadcasted_iota(jnp.int32, (tile, POOL_GW), 0)
    pooled = []
    for g, w in enumerate(POOL_WINDOWS):
        cols = slice(g * POOL_GW, (g + 1) * POOL_GW)
        s = u[:, cols]
        for back in range(1, w):
            s = s + pool_hist[POOL_PAD - back:POOL_PAD - back + tile, cols]
        cnt = jnp.minimum(w, pos1).astype(F32)
        pooled.append(s / cnt - u[:, cols])
    pooled = jnp.concatenate(pooled, axis=-1)

    ga = _dot(h, w_in_ref[:, 3 * D_MODEL:4 * D_MODEL])
    gb = _dot(h, w_in_ref[:, 4 * D_MODEL:5 * D_MODEL])
    o_ref[...] = _mixer_tail(x, gt_ref[pl.ds(b, 1), :], conv_out[...], pooled, ga, gb, cg_ref, cb_ref,
                             w_co_ref, pw_ref, ps_ref, w_po_ref, w_o_ref, g_ref, b_ref)

    @pl.when(t == pl.num_programs(1) - 1)
    def _():
        for j in range(N_LANE_TILES):
            nconv_ref[:, j * LANES:(j + 1) * LANES] = conv_hist[j, CONV_PAD + tile - CONV_HIST:CONV_PAD + tile, :]
        npool_ref[...] = pool_hist[POOL_PAD + tile - POOL_HIST:POOL_PAD + tile, :]


def _mixer_weight_specs():
    return [
        _const_spec((CONV_WIDTH, D_MODEL)),
        _const_spec((1, D_MODEL)),
        _const_spec((1, D_MODEL)),
        _const_spec((1, D_MODEL)),
        _const_spec((D_MODEL, D_MODEL)),
        _const_spec((len(POOL_WINDOWS), POOL_GW, POOL_GW)),
        _const_spec((1, D_MODEL)),
        _const_spec((D_MODEL, D_MODEL)),
        _const_spec((D_MODEL, D_MODEL)),
        _const_spec((1, D_MODEL)),
        _const_spec((1, D_MODEL)),
    ]


def _mixer_prompt(x, mod, w_in, tail_w, *, tile):
    nb, nt, _ = x.shape
    assert tile >= CONV_PAD and nt % tile == 0
    prompt_block = (mod.shape[1] - nb) // nb
    mod_specs = [pl.BlockSpec((None, nb, D_MODEL), functools.partial(lambda b, t, i: (i, prompt_block, 0), i=i))
                 for i in (4, 3, 5)]
    return pl.pallas_call(
        _mixer_kernel,
        grid=(nb, nt // tile),
        in_specs=[pl.BlockSpec((None, tile, D_MODEL), lambda b, t: (b, t, 0))] + mod_specs
        + [_const_spec((D_MODEL, 5 * D_MODEL))] + _mixer_weight_specs(),
        out_specs=[
            pl.BlockSpec((None, tile, D_MODEL), lambda b, t: (b, t, 0)),
            pl.BlockSpec((None, None, CONV_HIST, D_MODEL), lambda b, t: (0, b, 0, 0)),
            pl.BlockSpec((None, None, POOL_HIST, D_MODEL), lambda b, t: (0, b, 0, 0)),
        ],
        out_shape=[
            jax.ShapeDtypeStruct(x.shape, F32),
            jax.ShapeDtypeStruct((1, nb, CONV_HIST, D_MODEL), F32),
            jax.ShapeDtypeStruct((1, nb, POOL_HIST, D_MODEL), F32),
        ],
        scratch_shapes=[
            pltpu.VMEM((N_LANE_TILES, CONV_PAD + tile, LANES), F32),
            pltpu.VMEM((tile, D_MODEL), F32),
            pltpu.VMEM((POOL_PAD + tile, D_MODEL), F32),
        ],
        compiler_params=pltpu.CompilerParams(
            dimension_semantics=("arbitrary", "arbitrary"), vmem_limit_bytes=VMEM_LIMIT),
        name="mixer",
    )(x, mod, mod, mod, w_in, *tail_w)


def _sample_proj_kernel(x_ref, sc_ref, sh_ref, w_in_ref, glu_ref, u_ref, ga_ref, gb_ref):
    h = (x_ref[...] * (1.0 + sc_ref[...]) + sh_ref[...]).astype(BF16)
    a = _dot(h, w_in_ref[:, 0:D_MODEL])
    glu_ref[...] = a * jax.nn.sigmoid(_dot(h, w_in_ref[:, D_MODEL:2 * D_MODEL]))
    u_ref[...] = _dot(h, w_in_ref[:, 2 * D_MODEL:3 * D_MODEL])
    ga_ref[...] = _dot(h, w_in_ref[:, 3 * D_MODEL:4 * D_MODEL])
    gb_ref[...] = _dot(h, w_in_ref[:, 4 * D_MODEL:5 * D_MODEL])


def _sample_state_kernel(sconv_ref, spool_ref, glu_ref, u_ref, cw_ref, cbias_ref,
                         nconv_ref, npool_ref, conv_ref, pooled_ref):
    sconv = sconv_ref[...]
    glu = glu_ref[...]
    conv = jnp.sum(sconv * cw_ref[0:CONV_HIST, :][None], axis=1)
    conv_ref[...] = conv + glu * cw_ref[CONV_HIST:CONV_WIDTH, :] + cbias_ref[...]
    nconv_ref[:, 0:CONV_HIST - 1, :] = sconv[:, 1:CONV_HIST, :]
    nconv_ref[:, CONV_HIST - 1:CONV_HIST, :] = glu[:, None, :]

    spool = spool_ref[...]
    u = u_ref[...]
    pooled = []
    for g, w in enumerate(POOL_WINDOWS):
        cols = slice(g * POOL_GW, (g + 1) * POOL_GW)
        s = u[:, cols] + jnp.sum(spool[:, POOL_HIST - (w - 1):POOL_HIST, cols], axis=1)
        cnt = float(min(w, PAST_LEN + 1))
        pooled.append(s / cnt - u[:, cols])
    pooled_ref[...] = jnp.concatenate(pooled, axis=-1)
    npool_ref[:, 0:POOL_HIST - 1, :] = spool[:, 1:POOL_HIST, :]
    npool_ref[:, POOL_HIST - 1:POOL_HIST, :] = u[:, None, :]


def _sample_tail_kernel(x_ref, gt_ref, conv_ref, pooled_ref, ga_ref, gb_ref, cg_ref, cb_ref, w_co_ref,
                        pw_ref, ps_ref, w_po_ref, w_o_ref, g_ref, b_ref, o_ref):
    o_ref[...] = _mixer_tail(x_ref[...], gt_ref[...], conv_ref[...], pooled_ref[...], ga_ref[...], gb_ref[...],
                             cg_ref, cb_ref, w_co_ref, pw_ref, ps_ref, w_po_ref, w_o_ref, g_ref, b_ref)


def _mixer_sample(x, mod, state_conv, state_pool, w_in, tail_w):
    ns = x.shape[0]
    row_spec = pl.BlockSpec((ns, D_MODEL), lambda i: (0, 0))
    row_shape = jax.ShapeDtypeStruct((ns, D_MODEL), F32)

    def mod_spec(i):
        return pl.BlockSpec((None, ns, D_MODEL), lambda _: (i, 0, 0))

    glu, u, ga, gb = pl.pallas_call(
        _sample_proj_kernel,
        grid=(1,),
        in_specs=[row_spec, mod_spec(4), mod_spec(3), _const_spec((D_MODEL, 5 * D_MODEL))],
        out_specs=[row_spec] * 4,
        out_shape=[row_shape] * 4,
        compiler_params=pltpu.CompilerParams(dimension_semantics=("arbitrary",), vmem_limit_bytes=VMEM_LIMIT),
        name="sample_proj",
    )(x, mod, mod, w_in)

    conv_w, conv_b = tail_w[0], tail_w[1]
    sb = STATE_BATCH
    tok_spec = pl.BlockSpec((sb, D_MODEL), lambda i: (i, 0))
    nconv, npool, conv, pooled = pl.pallas_call(
        _sample_state_kernel,
        grid=(ns // sb,),
        in_specs=[
            pl.BlockSpec((sb, CONV_HIST, D_MODEL), lambda i: (i, 0, 0)),
            pl.BlockSpec((sb, POOL_HIST, D_MODEL), lambda i: (i, 0, 0)),
            tok_spec, tok_spec,
            _const_spec((CONV_WIDTH, D_MODEL)),
            _const_spec((1, D_MODEL)),
        ],
        out_specs=[
            pl.BlockSpec((sb, CONV_HIST, D_MODEL), lambda i: (i, 0, 0)),
            pl.BlockSpec((sb, POOL_HIST, D_MODEL), lambda i: (i, 0, 0)),
            tok_spec, tok_spec,
        ],
        out_shape=[
            jax.ShapeDtypeStruct((ns, CONV_HIST, D_MODEL), F32),
            jax.ShapeDtypeStruct((ns, POOL_HIST, D_MODEL), F32),
            row_shape, row_shape,
        ],
        compiler_params=pltpu.CompilerParams(dimension_semantics=("arbitrary",), vmem_limit_bytes=VMEM_LIMIT),
        name="sample_state",
    )(state_conv, state_pool, glu, u, conv_w, conv_b)

    out = pl.pallas_call(
        _sample_tail_kernel,
        grid=(1,),
        in_specs=[row_spec, mod_spec(5), row_spec, row_spec, row_spec, row_spec] + _mixer_weight_specs()[2:],
        out_specs=row_spec,
        out_shape=row_shape,
        compiler_params=pltpu.CompilerParams(dimension_semantics=("arbitrary",), vmem_limit_bytes=VMEM_LIMIT),
        name="sample_tail",
    )(x, mod, conv, pooled, ga, gb, *tail_w[2:])
    return out, nconv, npool


def kernel(x_prompt, x_sample, state_conv, state_pool, c_prompt, c_sample, w_ada, b_ada, ffn1_w_in, ffn1_w_out, ln1_g, ln1_b, w_in, conv_w, conv_b, conv_ln_g, conv_ln_b, w_conv_out, pool_w, pool_scale, w_pool_out, w_out, ln2_g, ln2_b, ffn2_w_in, ffn2_w_out, ln3_g, ln3_b):
    nb, nt, d = x_prompt.shape
    ns, st, _ = x_sample.shape
    assert d == D_MODEL and st == 1 and w_ada.shape[0] == DEPTH
    assert nt % TOKEN_TILE == 0 and ns % nb == 0 and ns % STATE_BATCH == 0

    mod = _ada(jnp.concatenate([c_sample, c_prompt], axis=0), w_ada[0], b_ada[0])

    row = lambda v: v[0].reshape(1, D_MODEL)
    ffn1 = (ffn1_w_in[0].astype(BF16), ffn1_w_out[0].astype(BF16), row(ln1_g), row(ln1_b))
    ffn2 = (ffn2_w_in[0].astype(BF16), ffn2_w_out[0].astype(BF16), row(ln3_g), row(ln3_b))
    w_in_b = w_in[0].astype(BF16)
    tail_w = (conv_w[0], row(conv_b), row(conv_ln_g), row(conv_ln_b), w_conv_out[0].astype(BF16),
              pool_w[0].astype(BF16), row(pool_scale), w_pool_out[0].astype(BF16), w_out[0].astype(BF16),
              row(ln2_g), row(ln2_b))

    xp = _ffn(x_prompt, mod, (1, 0, 2), *ffn1, per_row=False, tile=TOKEN_TILE)
    xp, nconv_p, npool_p = _mixer_prompt(xp, mod, w_in_b, tail_w, tile=TOKEN_TILE)
    xp = _ffn(xp, mod, (7, 6, 8), *ffn2, per_row=False, tile=TOKEN_TILE)

    xs = x_sample.reshape(1, ns, D_MODEL)
    xs = _ffn(xs, mod, (1, 0, 2), *ffn1, per_row=True, tile=ns)
    xs, nconv_s, npool_s = _mixer_sample(xs[0], mod, state_conv[0], state_pool[0], w_in_b, tail_w)
    xs = _ffn(xs.reshape(1, ns, D_MODEL), mod, (7, 6, 8), *ffn2, per_row=True, tile=ns)

    return (xp, xs.reshape(ns, 1, D_MODEL), nconv_p, npool_p, nconv_s[None], npool_s[None])
```

```python
import functools

import jax
import jax.numpy as jnp
from jax import lax
from jax.experimental import pallas as pl
from jax.experimental.pallas import tpu as pltpu

D_MODEL = 1024
CONV_WIDTH = 31
CONV_HIST = CONV_WIDTH - 1
POOL_WINDOWS = (2, 4, 8, 16)
POOL_GW = D_MODEL // len(POOL_WINDOWS)
POOL_MAX = 16
POOL_HIST = POOL_MAX - 1
D_FF = ((8 * D_MODEL // 3 + 127) // 128) * 128
N_MOD = 9
DEPTH = 1
DN_ALPHA = (2.0 * DEPTH) ** 0.25
FFN_RES = 0.5
LN_EPS = 1e-5
PAST_LEN = 16384

SUBLANES = 8
LANES = 128
BF16_ROWS = 2 * SUBLANES
N_LANE_TILES = D_MODEL // LANES
FF_CHUNK = 256
TOKEN_TILE = 512
HIST_PAD = 32
CONV_ROWS = 64
VMEM_LIMIT = 56 * 1024 * 1024

BF16 = jnp.bfloat16
F32 = jnp.float32
U32 = jnp.uint32

assert all(w & (w - 1) == 0 for w in POOL_WINDOWS) and POOL_GW == 2 * LANES
assert HIST_PAD >= CONV_HIST and HIST_PAD % BF16_ROWS == 0


def _const_spec(shape):
    n = len(shape)
    return pl.BlockSpec(shape, lambda *_: (0,) * n, pipeline_mode=pl.Buffered(1))


def _dot(a, b):
    return jnp.dot(a, b, preferred_element_type=F32)


def _layer_norm(z, g, b):
    mu = jnp.mean(z, axis=-1, keepdims=True)
    zc = z - mu
    var = jnp.mean(zc * zc, axis=-1, keepdims=True)
    return zc * lax.rsqrt(var + LN_EPS) * g + b


def _mod_rows(ref, per_row, b):
    if per_row:
        return ref[...]
    return ref[pl.ds(b, 1), :]


def _lane_tile(j):
    return slice(j * LANES, (j + 1) * LANES)


def _ada_kernel(c_ref, w_ref, b_ref, o_ref):
    c = c_ref[...]
    s = (c * jax.nn.sigmoid(c)).astype(BF16)
    o_ref[...] = _dot(s, w_ref[...].astype(BF16)) + b_ref[...]


def _ada(c_all, w_ada, b_ada):
    rows = c_all.shape[0]
    return pl.pallas_call(
        _ada_kernel,
        grid=(N_MOD,),
        in_specs=[
            pl.BlockSpec((rows, D_MODEL), lambda i: (0, 0)),
            pl.BlockSpec((D_MODEL, D_MODEL), lambda i: (0, i)),
            pl.BlockSpec((1, D_MODEL), lambda i: (0, i)),
        ],
        out_specs=pl.BlockSpec((None, rows, D_MODEL), lambda i: (i, 0, 0)),
        out_shape=jax.ShapeDtypeStruct((N_MOD, rows, D_MODEL), F32),
        compiler_params=pltpu.CompilerParams(dimension_semantics=("arbitrary",)),
        name="ada",
    )(c_all, w_ada, b_ada.reshape(1, N_MOD * D_MODEL))


def _ffn_kernel(per_row, x_ref, sc_ref, sh_ref, gt_ref, w_in_ref, w_out_ref, g_ref, b_ref, o_ref):
    b = pl.program_id(0)
    x = x_ref[...]
    h = (x * (1.0 + _mod_rows(sc_ref, per_row, b)) + _mod_rows(sh_ref, per_row, b)).astype(BF16)
    acc = jnp.zeros(x.shape, F32)
    for c0 in range(0, D_FF, FF_CHUNK):
        gate = _dot(h, w_in_ref[:, c0:c0 + FF_CHUNK])
        up = _dot(h, w_in_ref[:, D_FF + c0:D_FF + c0 + FF_CHUNK])
        act = (gate * jax.nn.sigmoid(gate) * up).astype(BF16)
        acc = acc + _dot(act, w_out_ref[c0:c0 + FF_CHUNK, :])
    z = DN_ALPHA * x + FFN_RES * _mod_rows(gt_ref, per_row, b) * acc
    o_ref[...] = _layer_norm(z, g_ref[...], b_ref[...])


def _mod_spec(per_row, rows, idx, prompt_block):
    if per_row:
        return pl.BlockSpec((None, rows, D_MODEL), lambda b, t: (idx, 0, 0))
    return pl.BlockSpec((None, rows, D_MODEL), lambda b, t: (idx, prompt_block, 0))


def _ffn(x, mod, mod_idx, w_in, w_out, ln_g, ln_b, *, per_row, tile):
    nb, nt, _ = x.shape
    mod_rows = tile if per_row else nb
    prompt_block = None if per_row else (mod.shape[1] - nb) // nb
    specs = [_mod_spec(per_row, mod_rows, i, prompt_block) for i in mod_idx]
    return pl.pallas_call(
        functools.partial(_ffn_kernel, per_row),
        grid=(nb, nt // tile),
        in_specs=[pl.BlockSpec((None, tile, D_MODEL), lambda b, t: (b, t, 0))] + specs + [
            _const_spec((D_MODEL, 2 * D_FF)),
            _const_spec((D_FF, D_MODEL)),
            _const_spec((1, D_MODEL)),
            _const_spec((1, D_MODEL)),
        ],
        out_specs=pl.BlockSpec((None, tile, D_MODEL), lambda b, t: (b, t, 0)),
        out_shape=jax.ShapeDtypeStruct(x.shape, F32),
        compiler_params=pltpu.CompilerParams(
            dimension_semantics=("arbitrary", "arbitrary"), vmem_limit_bytes=VMEM_LIMIT),
        name="ffn_rows" if per_row else "ffn",
    )(x, mod, mod, mod, w_in, w_out, ln_g, ln_b)


def _mixer_tail(x, gate_rows, conv, pooled, ga, gb, cg_ref, cb_ref, w_co_ref, pw_ref, ps_ref,
                w_po_ref, w_o_ref, g_ref, b_ref):
    ya_in = _layer_norm(conv, cg_ref[...], cb_ref[...])
    ya_in = (ya_in * jax.nn.sigmoid(ya_in)).astype(BF16)
    merged = jax.nn.sigmoid(ga) * _dot(ya_in, w_co_ref[...])
    pooled = pooled.astype(BF16)
    mixed = jnp.concatenate(
        [_dot(pooled[:, g * POOL_GW:(g + 1) * POOL_GW], pw_ref[g]) for g in range(len(POOL_WINDOWS))],
        axis=-1)
    yb = _dot((mixed * ps_ref[...]).astype(BF16), w_po_ref[...])
    merged = (merged + jax.nn.sigmoid(gb) * yb).astype(BF16)
    z = DN_ALPHA * x + gate_rows * _dot(merged, w_o_ref[...])
    return _layer_norm(z, g_ref[...], b_ref[...])


def _window_sum(hist, work, j, w, tile):
    end = HIST_PAD + tile
    lo = {w: HIST_PAD}
    v = w
    while v > 2:
        lo[v // 2] = (lo[v] - v // 2) // SUBLANES * SUBLANES
        v //= 2
    s = hist[j, lo[2]:end, :] + hist[j, lo[2] - 1:end - 1, :]
    v, slot = 2, 0
    while v < w:
        work[j, slot, lo[v]:end, :] = s
        s = work[j, slot, lo[2 * v]:end, :] + work[j, slot, lo[2 * v] - v:end - v, :]
        v, slot = 2 * v, 1 - slot
    return s


def _pack_rows(rows_f32):
    return pltpu.bitcast(rows_f32.astype(BF16), U32)


def _conv_chunk(conv_even, conv_odd, conv_wb, j, q0):
    n = CONV_ROWS // 2
    taps, weights = [], []
    for k in range(CONV_WIDTH):
        back = CONV_HIST - k
        if back % 2 == 0:
            words = conv_even[j, pl.ds(q0 - back // 2, n), :]
        else:
            words = conv_odd[j, pl.ds(q0 - (back - 1) // 2, n), :]
        taps.append(pltpu.bitcast(words, BF16))
        weights.append(jnp.concatenate([conv_wb[j, k]] * (CONV_ROWS // BF16_ROWS), axis=0))
    taps = jnp.stack(taps, axis=0).astype(F32)
    weights = jnp.stack(weights, axis=0).astype(F32)
    return jnp.sum(taps * weights, axis=0)


def _mixer_kernel(x_ref, sc_ref, sh_ref, gt_ref, w_in_ref, cw_ref, cbias_ref, cg_ref, cb_ref,
                  w_co_ref, pw_ref, ps_ref, w_po_ref, w_o_ref, g_ref, b_ref,
                  o_ref, nconv_ref, npool_ref, conv_hist, conv_even, conv_odd, conv_wb, conv_out, pool_hist, work):
    b = pl.program_id(0)
    t = pl.program_id(1)
    tile = x_ref.shape[0]
    hist_words = HIST_PAD // 2
    tile_words = tile // 2

    @pl.when((b == 0) & (t == 0))
    def _():
        for j in range(N_LANE_TILES):
            for k in range(CONV_WIDTH):
                conv_wb[j, k] = jnp.broadcast_to(cw_ref[k:k + 1, _lane_tile(j)], (BF16_ROWS, LANES)).astype(BF16)

    @pl.when(t == 0)
    def _():
        conv_hist[:, 0:HIST_PAD, :] = jnp.zeros((N_LANE_TILES, HIST_PAD, LANES), F32)
        pool_hist[:, 0:HIST_PAD, :] = jnp.zeros((N_LANE_TILES, HIST_PAD, LANES), F32)
        conv_even[:, 0:hist_words, :] = jnp.zeros((N_LANE_TILES, hist_words, LANES), U32)
        conv_odd[:, 0:hist_words, :] = jnp.zeros((N_LANE_TILES, hist_words, LANES), U32)

    @pl.when(t > 0)
    def _():
        conv_hist[:, 0:HIST_PAD, :] = conv_hist[:, tile:tile + HIST_PAD, :]
        pool_hist[:, 0:HIST_PAD, :] = pool_hist[:, tile:tile + HIST_PAD, :]
        conv_even[:, 0:hist_words, :] = conv_even[:, tile_words:tile_words + hist_words, :]
        conv_odd[:, 0:hist_words, :] = conv_odd[:, tile_words:tile_words + hist_words, :]

    x = x_ref[...]
    h = (x * (1.0 + sc_ref[pl.ds(b, 1), :]) + sh_ref[pl.ds(b, 1), :]).astype(BF16)

    glu = _dot(h, w_in_ref[:, 0:D_MODEL])
    glu = glu * jax.nn.sigmoid(_dot(h, w_in_ref[:, D_MODEL:2 * D_MODEL]))
    for j in range(N_LANE_TILES):
        rows = glu[:, _lane_tile(j)]
        conv_hist[j, HIST_PAD:HIST_PAD + tile, :] = rows
        conv_even[j, hist_words:hist_words + tile_words, :] = _pack_rows(rows)
    for j in range(N_LANE_TILES):
        conv_odd[j, hist_words:hist_words + tile_words, :] = _pack_rows(
            conv_hist[j, HIST_PAD - 1:HIST_PAD - 1 + tile, :])

    u = _dot(h, w_in_ref[:, 2 * D_MODEL:3 * D_MODEL])
    for j in range(N_LANE_TILES):
        pool_hist[j, HIST_PAD:HIST_PAD + tile, :] = u[:, _lane_tile(j)]
    ga = _dot(h, w_in_ref[:, 3 * D_MODEL:4 * D_MODEL])
    gb = _dot(h, w_in_ref[:, 4 * D_MODEL:5 * D_MODEL])

    def conv_rows(c, carry):
        r0 = pl.multiple_of(c * CONV_ROWS, CONV_ROWS)
        q0 = hist_words + pl.multiple_of(c * (CONV_ROWS // 2), CONV_ROWS // 2)
        for j in range(N_LANE_TILES):
            conv_out[j, pl.ds(r0, CONV_ROWS), :] = _conv_chunk(conv_even, conv_odd, conv_wb, j, q0)
        return carry

    lax.fori_loop(0, tile // CONV_ROWS, conv_rows, 0)
    conv = jnp.concatenate([conv_out[j] + cbias_ref[:, _lane_tile(j)] for j in range(N_LANE_TILES)], axis=-1)

    pos1 = t * tile + 1 + lax.broadcasted_iota(jnp.int32, (POOL_MAX, LANES), 0)
    pooled = []
    for j in range(N_LANE_TILES):
        w = POOL_WINDOWS[j * LANES // POOL_GW]
        s = _window_sum(pool_hist, work, j, w, tile)
        head = s[0:POOL_MAX] / jnp.minimum(w, pos1).astype(F32)
        mean = jnp.concatenate([head, s[POOL_MAX:] * (1.0 / w)], axis=0)
        pooled.append(mean - u[:, _lane_tile(j)])
    pooled = jnp.concatenate(pooled, axis=-1)

    o_ref[...] = _mixer_tail(x, gt_ref[pl.ds(b, 1), :], conv, pooled, ga, gb, cg_ref, cb_ref,
                             w_co_ref, pw_ref, ps_ref, w_po_ref, w_o_ref, g_ref, b_ref)

    @pl.when((b == 0) & (t == 0))
    def _():
        nconv_ref[...] = jnp.zeros(nconv_ref.shape, F32)
        npool_ref[...] = jnp.zeros(npool_ref.shape, F32)

    @pl.when(t == pl.num_programs(1) - 1)
    def _():
        nb = nconv_ref.shape[1]
        mine = lax.broadcasted_iota(jnp.int32, (nb, LANES), 0) == b
        last = HIST_PAD + tile
        for hist, out_ref, n_rows in ((conv_hist, nconv_ref, CONV_HIST), (pool_hist, npool_ref, POOL_HIST)):
            for j in range(N_LANE_TILES):
                for r in range(n_rows):
                    src = last - n_rows + r
                    new = jnp.broadcast_to(hist[j, src:src + 1, :], (nb, LANES))
                    out_ref[r, :, _lane_tile(j)] = jnp.where(mine, new, out_ref[r, :, _lane_tile(j)])


def _mixer_weight_specs():
    return [
        _const_spec((CONV_WIDTH, D_MODEL)),
        _const_spec((1, D_MODEL)),
        _const_spec((1, D_MODEL)),
        _const_spec((1, D_MODEL)),
        _const_spec((D_MODEL, D_MODEL)),
        _const_spec((len(POOL_WINDOWS), POOL_GW, POOL_GW)),
        _const_spec((1, D_MODEL)),
        _const_spec((D_MODEL, D_MODEL)),
        _const_spec((D_MODEL, D_MODEL)),
        _const_spec((1, D_MODEL)),
        _const_spec((1, D_MODEL)),
    ]


def _mixer_prompt(x, mod, w_in, tail_w, *, tile):
    nb, nt, _ = x.shape
    assert tile >= HIST_PAD and tile % CONV_ROWS == 0 and nt % tile == 0
    prompt_block = (mod.shape[1] - nb) // nb
    mod_specs = [pl.BlockSpec((None, nb, D_MODEL), functools.partial(lambda b, t, i: (i, prompt_block, 0), i=i))
                 for i in (4, 3, 5)]
    hist = (N_LANE_TILES, HIST_PAD + tile, LANES)
    hist_packed = (N_LANE_TILES, (HIST_PAD + tile) // 2, LANES)
    return pl.pallas_call(
        _mixer_kernel,
        grid=(nb, nt // tile),
        in_specs=[pl.BlockSpec((None, tile, D_MODEL), lambda b, t: (b, t, 0))] + mod_specs
        + [_const_spec((D_MODEL, 5 * D_MODEL))] + _mixer_weight_specs(),
        out_specs=[
            pl.BlockSpec((None, tile, D_MODEL), lambda b, t: (b, t, 0)),
            pl.BlockSpec((CONV_HIST, nb, D_MODEL), lambda b, t: (0, 0, 0)),
            pl.BlockSpec((POOL_HIST, nb, D_MODEL), lambda b, t: (0, 0, 0)),
        ],
        out_shape=[
            jax.ShapeDtypeStruct(x.shape, F32),
            jax.ShapeDtypeStruct((CONV_HIST, nb, D_MODEL), F32),
            jax.ShapeDtypeStruct((POOL_HIST, nb, D_MODEL), F32),
        ],
        scratch_shapes=[
            pltpu.VMEM(hist, F32),
            pltpu.VMEM(hist_packed, U32),
            pltpu.VMEM(hist_packed, U32),
            pltpu.VMEM((N_LANE_TILES, CONV_WIDTH, BF16_ROWS, LANES), BF16),
            pltpu.VMEM((N_LANE_TILES, tile, LANES), F32),
            pltpu.VMEM(hist, F32),
            pltpu.VMEM((N_LANE_TILES, 2) + hist[1:], F32),
        ],
        compiler_params=pltpu.CompilerParams(
            dimension_semantics=("arbitrary", "arbitrary"), vmem_limit_bytes=VMEM_LIMIT),
        name="mixer",
    )(x, mod, mod, mod, w_in, *tail_w)


def _sample_proj_kernel(x_ref, sc_ref, sh_ref, w_in_ref, glu_ref, u_ref, ga_ref, gb_ref):
    h = (x_ref[...] * (1.0 + sc_ref[...]) + sh_ref[...]).astype(BF16)
    a = _dot(h, w_in_ref[:, 0:D_MODEL])
    glu_ref[...] = a * jax.nn.sigmoid(_dot(h, w_in_ref[:, D_MODEL:2 * D_MODEL]))
    u_ref[...] = _dot(h, w_in_ref[:, 2 * D_MODEL:3 * D_MODEL])
    ga_ref[...] = _dot(h, w_in_ref[:, 3 * D_MODEL:4 * D_MODEL])
    gb_ref[...] = _dot(h, w_in_ref[:, 4 * D_MODEL:5 * D_MODEL])


def _sample_state_kernel(sconv_ref, spool_ref, glu_ref, u_ref, cw_ref, cbias_ref,
                         nconv_ref, npool_ref, conv_ref, pooled_ref):
    glu = glu_ref[...]
    conv = cbias_ref[...] + glu * cw_ref[CONV_HIST:CONV_WIDTH, :]
    for k in range(CONV_HIST):
        conv = conv + sconv_ref[k] * cw_ref[k:k + 1, :]
    conv_ref[...] = conv
    nconv_ref[0:CONV_HIST - 1] = sconv_ref[1:CONV_HIST]
    nconv_ref[CONV_HIST - 1] = glu

    u = u_ref[...]
    w = jnp.left_shift(2, pl.program_id(0))
    s = u
    for back in range(1, POOL_MAX):
        s = s + jnp.where(back < w, spool_ref[POOL_HIST - back], 0.0)
    pooled_ref[...] = s / jnp.minimum(w, PAST_LEN + 1).astype(F32) - u
    npool_ref[0:POOL_HIST - 1] = spool_ref[1:POOL_HIST]
    npool_ref[POOL_HIST - 1] = u


def _sample_tail_kernel(x_ref, gt_ref, conv_ref, pooled_ref, ga_ref, gb_ref, cg_ref, cb_ref, w_co_ref,
                        pw_ref, ps_ref, w_po_ref, w_o_ref, g_ref, b_ref, o_ref):
    o_ref[...] = _mixer_tail(x_ref[...], gt_ref[...], conv_ref[...], pooled_ref[...], ga_ref[...], gb_ref[...],
                             cg_ref, cb_ref, w_co_ref, pw_ref, ps_ref, w_po_ref, w_o_ref, g_ref, b_ref)


def _mixer_sample(x, mod, state_conv, state_pool, w_in, tail_w):
    ns = x.shape[0]
    row_spec = pl.BlockSpec((ns, D_MODEL), lambda i: (0, 0))
    row_shape = jax.ShapeDtypeStruct((ns, D_MODEL), F32)

    def mod_spec(i):
        return pl.BlockSpec((None, ns, D_MODEL), lambda _: (i, 0, 0))

    glu, u, ga, gb = pl.pallas_call(
        _sample_proj_kernel,
        grid=(1,),
        in_specs=[row_spec, mod_spec(4), mod_spec(3), _const_spec((D_MODEL, 5 * D_MODEL))],
        out_specs=[row_spec] * 4,
        out_shape=[row_shape] * 4,
        compiler_params=pltpu.CompilerParams(dimension_semantics=("arbitrary",), vmem_limit_bytes=VMEM_LIMIT),
        name="sample_proj",
    )(x, mod, mod, w_in)

    conv_w, conv_b = tail_w[0], tail_w[1]
    tok_spec = pl.BlockSpec((ns, POOL_GW), lambda g: (0, g))
    nconv, npool, conv, pooled = pl.pallas_call(
        _sample_state_kernel,
        grid=(len(POOL_WINDOWS),),
        in_specs=[
            pl.BlockSpec((CONV_HIST, ns, POOL_GW), lambda g: (0, 0, g)),
            pl.BlockSpec((POOL_HIST, ns, POOL_GW), lambda g: (0, 0, g)),
            tok_spec, tok_spec,
            pl.BlockSpec((CONV_WIDTH, POOL_GW), lambda g: (0, g)),
            pl.BlockSpec((1, POOL_GW), lambda g: (0, g)),
        ],
        out_specs=[
            pl.BlockSpec((CONV_HIST, ns, POOL_GW), lambda g: (0, 0, g)),
            pl.BlockSpec((POOL_HIST, ns, POOL_GW), lambda g: (0, 0, g)),
            tok_spec, tok_spec,
        ],
        out_shape=[
            jax.ShapeDtypeStruct((CONV_HIST, ns, D_MODEL), F32),
            jax.ShapeDtypeStruct((POOL_HIST, ns, D_MODEL), F32),
            row_shape, row_shape,
        ],
        compiler_params=pltpu.CompilerParams(dimension_semantics=("arbitrary",), vmem_limit_bytes=VMEM_LIMIT),
        name="sample_state",
    )(state_conv, state_pool, glu, u, conv_w, conv_b)

    out = pl.pallas_call(
        _sample_tail_kernel,
        grid=(1,),
        in_specs=[row_spec, mod_spec(5), row_spec, row_spec, row_spec, row_spec] + _mixer_weight_specs()[2:],
        out_specs=row_spec,
        out_shape=row_shape,
        compiler_params=pltpu.CompilerParams(dimension_semantics=("arbitrary",), vmem_limit_bytes=VMEM_LIMIT),
        name="sample_tail",
    )(x, mod, conv, pooled, ga, gb, *tail_w[2:])
    return out, nconv, npool


def _time_major(state):
    return jnp.transpose(state, (0, 2, 1, 3))[0]


def _batch_major(state):
    return jnp.transpose(state[None], (0, 2, 1, 3))


def kernel(x_prompt, x_sample, state_conv, state_pool, c_prompt, c_sample, w_ada, b_ada, ffn1_w_in, ffn1_w_out, ln1_g, ln1_b, w_in, conv_w, conv_b, conv_ln_g, conv_ln_b, w_conv_out, pool_w, pool_scale, w_pool_out, w_out, ln2_g, ln2_b, ffn2_w_in, ffn2_w_out, ln3_g, ln3_b):
    nb, nt, d = x_prompt.shape
    ns, st, _ = x_sample.shape
    assert d == D_MODEL and st == 1 and w_ada.shape[0] == DEPTH
    assert nt % TOKEN_TILE == 0 and ns % nb == 0

    mod = _ada(jnp.concatenate([c_sample, c_prompt], axis=0), w_ada[0], b_ada[0])

    row = lambda v: v[0].reshape(1, D_MODEL)
    ffn1 = (ffn1_w_in[0].astype(BF16), ffn1_w_out[0].astype(BF16), row(ln1_g), row(ln1_b))
    ffn2 = (ffn2_w_in[0].astype(BF16), ffn2_w_out[0].astype(BF16), row(ln3_g), row(ln3_b))
    w_in_b = w_in[0].astype(BF16)
    tail_w = (conv_w[0], row(conv_b), row(conv_ln_g), row(conv_ln_b), w_conv_out[0].astype(BF16),
              pool_w[0].astype(BF16), row(pool_scale), w_pool_out[0].astype(BF16), w_out[0].astype(BF16),
              row(ln2_g), row(ln2_b))

    xp = _ffn(x_prompt, mod, (1, 0, 2), *ffn1, per_row=False, tile=TOKEN_TILE)
    xp, nconv_p, npool_p = _mixer_prompt(xp, mod, w_in_b, tail_w, tile=TOKEN_TILE)
    xp = _ffn(xp, mod, (7, 6, 8), *ffn2, per_row=False, tile=TOKEN_TILE)

    xs = x_sample.reshape(1, ns, D_MODEL)
    xs = _ffn(xs, mod, (1, 0, 2), *ffn1, per_row=True, tile=ns)
    xs, nconv_s, npool_s = _mixer_sample(xs[0], mod, _time_major(state_conv), _time_major(state_pool),
                                         w_in_b, tail_w)
    xs = _ffn(xs.reshape(1, ns, D_MODEL), mod, (7, 6, 8), *ffn2, per_row=True, tile=ns)

    return (xp, xs.reshape(ns, 1, D_MODEL), _batch_major(nconv_p), _batch_major(npool_p),
            _batch_major(nconv_s), _batch_major(npool_s))
```

```python
import functools

import jax
import jax.numpy as jnp
from jax import lax
from jax.experimental import pallas as pl
from jax.experimental.pallas import tpu as pltpu

D_MODEL = 1024
CONV_WIDTH = 31
CONV_HIST = CONV_WIDTH - 1
POOL_WINDOWS = (2, 4, 8, 16)
POOL_GW = D_MODEL // len(POOL_WINDOWS)
POOL_MAX = 16
POOL_HIST = POOL_MAX - 1
D_FF = ((8 * D_MODEL // 3 + 127) // 128) * 128
N_MOD = 9
DEPTH = 1
DN_ALPHA = (2.0 * DEPTH) ** 0.25
FFN_RES = 0.5
LN_EPS = 1e-5
PAST_LEN = 16384

SUBLANES = 8
LANES = 128
BF16_ROWS = 2 * SUBLANES
N_LANE_TILES = D_MODEL // LANES
FF_CHUNK = 256
N_FF_CHUNKS = D_FF // FF_CHUNK
FFN_STAGE_SLOTS = 2
TOKEN_TILE = 512
HIST_PAD = 32
CONV_ROWS = 64
VMEM_LIMIT = 56 * 1024 * 1024

BF16 = jnp.bfloat16
F32 = jnp.float32
U32 = jnp.uint32

assert all(w & (w - 1) == 0 for w in POOL_WINDOWS) and POOL_GW == 2 * LANES
assert HIST_PAD >= CONV_HIST and HIST_PAD % BF16_ROWS == 0


def _const_spec(shape):
    n = len(shape)
    return pl.BlockSpec(shape, lambda *_: (0,) * n, pipeline_mode=pl.Buffered(1))


def _dot(a, b):
    return jnp.dot(a, b, preferred_element_type=F32)


def _layer_norm(z, g, b):
    mu = jnp.mean(z, axis=-1, keepdims=True)
    zc = z - mu
    var = jnp.mean(zc * zc, axis=-1, keepdims=True)
    return zc * lax.rsqrt(var + LN_EPS) * g + b


def _lane_tile(j):
    return slice(j * LANES, (j + 1) * LANES)


def _ada_kernel(c_ref, w_ref, b_ref, o_ref):
    c = c_ref[...]
    s = (c * jax.nn.sigmoid(c)).astype(BF16)
    o_ref[...] = _dot(s, w_ref[...].astype(BF16)) + b_ref[...]


def _ada(c_all, w_ada, b_ada):
    rows = c_all.shape[0]
    return pl.pallas_call(
        _ada_kernel,
        grid=(N_MOD,),
        in_specs=[
            pl.BlockSpec((rows, D_MODEL), lambda i: (0, 0)),
            pl.BlockSpec((D_MODEL, D_MODEL), lambda i: (0, i)),
            pl.BlockSpec((1, D_MODEL), lambda i: (0, i)),
        ],
        out_specs=pl.BlockSpec((None, rows, D_MODEL), lambda i: (i, 0, 0)),
        out_shape=jax.ShapeDtypeStruct((N_MOD, rows, D_MODEL), F32),
        compiler_params=pltpu.CompilerParams(dimension_semantics=("arbitrary",)),
        name="ada",
    )(c_all, w_ada, b_ada.reshape(1, N_MOD * D_MODEL))


def _ffn_chunk_copies(c, slot, w_in_hbm, w_out_hbm, stage_in, stage_out, sems):
    c0 = c * FF_CHUNK
    return (
        pltpu.make_async_copy(w_in_hbm.at[:, pl.ds(c0, FF_CHUNK)], stage_in.at[slot, 0], sems.at[slot, 0]),
        pltpu.make_async_copy(w_in_hbm.at[:, pl.ds(D_FF + c0, FF_CHUNK)], stage_in.at[slot, 1], sems.at[slot, 1]),
        pltpu.make_async_copy(w_out_hbm.at[pl.ds(c0, FF_CHUNK), :], stage_out.at[slot], sems.at[slot, 2]),
    )


def _ffn_tile(x, sc, sh, gt, w_in_bf, w_out_bf, g_ref, b_ref, prepare_chunk):
    h = (x * (1.0 + sc) + sh).astype(BF16)
    acc = jnp.zeros(x.shape, F32)
    for c in range(N_FF_CHUNKS):
        prepare_chunk(c)
        c0 = c * FF_CHUNK
        gate = _dot(h, w_in_bf[:, c0:c0 + FF_CHUNK])
        up = _dot(h, w_in_bf[:, D_FF + c0:D_FF + c0 + FF_CHUNK])
        act = (gate * jax.nn.sigmoid(gate) * up).astype(BF16)
        acc = acc + _dot(act, w_out_bf[c0:c0 + FF_CHUNK, :])
    z = DN_ALPHA * x + FFN_RES * gt * acc
    return _layer_norm(z, g_ref[...], b_ref[...])


def _ffn_kernel(tiles_per_seq, n_prompt_steps,
                x_ref, xs_ref, sc_ref, sh_ref, gt_ref, scs_ref, shs_ref, gts_ref, w_in_hbm, w_out_hbm, g_ref, b_ref,
                o_ref, os_ref, w_in_bf, w_out_bf, stage_in, stage_out, sems):
    s = pl.program_id(0)
    copies = functools.partial(_ffn_chunk_copies, w_in_hbm=w_in_hbm, w_out_hbm=w_out_hbm,
                               stage_in=stage_in, stage_out=stage_out, sems=sems)

    def load_chunk(c):
        slot = c % FFN_STAGE_SLOTS
        c0 = c * FF_CHUNK
        for copy in copies(c, slot):
            copy.wait()
        w_in_bf[:, c0:c0 + FF_CHUNK] = stage_in[slot, 0].astype(BF16)
        w_in_bf[:, D_FF + c0:D_FF + c0 + FF_CHUNK] = stage_in[slot, 1].astype(BF16)
        w_out_bf[c0:c0 + FF_CHUNK, :] = stage_out[slot].astype(BF16)
        if c + FFN_STAGE_SLOTS < N_FF_CHUNKS:
            for copy in copies(c + FFN_STAGE_SLOTS, slot):
                copy.start()

    def prompt_tile(prepare_chunk):
        seq = s // tiles_per_seq
        o_ref[...] = _ffn_tile(x_ref[...], sc_ref[pl.ds(seq, 1), :], sh_ref[pl.ds(seq, 1), :],
                               gt_ref[pl.ds(seq, 1), :], w_in_bf, w_out_bf, g_ref, b_ref, prepare_chunk)

    @pl.when(s == 0)
    def _():
        for c in range(FFN_STAGE_SLOTS):
            for copy in copies(c, c):
                copy.start()
        prompt_tile(load_chunk)

    @pl.when((s > 0) & (s < n_prompt_steps))
    def _():
        prompt_tile(lambda c: None)

    @pl.when(s == n_prompt_steps)
    def _():
        os_ref[...] = _ffn_tile(xs_ref[...], scs_ref[...], shs_ref[...], gts_ref[...],
                                w_in_bf, w_out_bf, g_ref, b_ref, lambda c: None)


def _ffn(x, xs, mod, mod_idx, w_in, w_out, ln_g, ln_b, *, tile):
    nb, nt, _ = x.shape
    ns = xs.shape[0]
    tiles_per_seq = nt // tile
    n_prompt_steps = nb * tiles_per_seq
    prompt_block = ns // nb

    def tile_index(s):
        p = jnp.minimum(s, n_prompt_steps - 1)
        return (p // tiles_per_seq, p % tiles_per_seq, 0)

    tile_spec = pl.BlockSpec((None, tile, D_MODEL), tile_index)
    rows_spec = pl.BlockSpec((ns, D_MODEL), lambda s: (0, 0))
    prompt_mod = [pl.BlockSpec((None, nb, D_MODEL), functools.partial(lambda s, i: (i, prompt_block, 0), i=i))
                  for i in mod_idx]
    sample_mod = [pl.BlockSpec((None, ns, D_MODEL), functools.partial(lambda s, i: (i, 0, 0), i=i))
                  for i in mod_idx]
    return pl.pallas_call(
        functools.partial(_ffn_kernel, tiles_per_seq, n_prompt_steps),
        grid=(n_prompt_steps + 1,),
        in_specs=[tile_spec, rows_spec] + prompt_mod + sample_mod + [
            pl.BlockSpec(memory_space=pl.ANY),
            pl.BlockSpec(memory_space=pl.ANY),
            _const_spec((1, D_MODEL)),
            _const_spec((1, D_MODEL)),
        ],
        out_specs=[tile_spec, rows_spec],
        out_shape=[jax.ShapeDtypeStruct(x.shape, F32), jax.ShapeDtypeStruct(xs.shape, F32)],
        scratch_shapes=[
            pltpu.VMEM((D_MODEL, 2 * D_FF), BF16),
            pltpu.VMEM((D_FF, D_MODEL), BF16),
            pltpu.VMEM((FFN_STAGE_SLOTS, 2, D_MODEL, FF_CHUNK), F32),
            pltpu.VMEM((FFN_STAGE_SLOTS, FF_CHUNK, D_MODEL), F32),
            pltpu.SemaphoreType.DMA((FFN_STAGE_SLOTS, 3)),
        ],
        compiler_params=pltpu.CompilerParams(dimension_semantics=("arbitrary",), vmem_limit_bytes=VMEM_LIMIT),
        name="ffn",
    )(x, xs, mod, mod, mod, mod, mod, mod, w_in, w_out, ln_g, ln_b)


def _mixer_tail(x, gate_rows, conv, pooled, ga, gb, cg_ref, cb_ref, w_co_ref, pw_ref, ps_ref,
                w_po_ref, w_o_ref, g_ref, b_ref):
    ya_in = _layer_norm(conv, cg_ref[...], cb_ref[...])
    ya_in = (ya_in * jax.nn.sigmoid(ya_in)).astype(BF16)
    merged = jax.nn.sigmoid(ga) * _dot(ya_in, w_co_ref[...])
    pooled = pooled.astype(BF16)
    mixed = jnp.concatenate(
        [_dot(pooled[:, g * POOL_GW:(g + 1) * POOL_GW], pw_ref[g]) for g in range(len(POOL_WINDOWS))],
        axis=-1)
    yb = _dot((mixed * ps_ref[...]).astype(BF16), w_po_ref[...])
    merged = (merged + jax.nn.sigmoid(gb) * yb).astype(BF16)
    z = DN_ALPHA * x + gate_rows * _dot(merged, w_o_ref[...])
    return _layer_norm(z, g_ref[...], b_ref[...])


def _window_sum(hist, work, j, w, tile):
    end = HIST_PAD + tile
    lo = {w: HIST_PAD}
    v = w
    while v > 2:
        lo[v // 2] = (lo[v] - v // 2) // SUBLANES * SUBLANES
        v //= 2
    s = hist[j, lo[2]:end, :] + hist[j, lo[2] - 1:end - 1, :]
    v, slot = 2, 0
    while v < w:
        work[j, slot, lo[v]:end, :] = s
        s = work[j, slot, lo[2 * v]:end, :] + work[j, slot, lo[2 * v] - v:end - v, :]
        v, slot = 2 * v, 1 - slot
    return s


def _pack_rows(rows_f32):
    return pltpu.bitcast(rows_f32.astype(BF16), U32)


def _conv_chunk(conv_even, conv_odd, conv_wb, j, q0):
    n = CONV_ROWS // 2
    taps, weights = [], []
    for k in range(CONV_WIDTH):
        back = CONV_HIST - k
        if back % 2 == 0:
            words = conv_even[j, pl.ds(q0 - back // 2, n), :]
        else:
            words = conv_odd[j, pl.ds(q0 - (back - 1) // 2, n), :]
        taps.append(pltpu.bitcast(words, BF16))
        weights.append(jnp.concatenate([conv_wb[j, k]] * (CONV_ROWS // BF16_ROWS), axis=0))
    taps = jnp.stack(taps, axis=0).astype(F32)
    weights = jnp.stack(weights, axis=0).astype(F32)
    return jnp.sum(taps * weights, axis=0)


def _mixer_kernel(x_ref, sc_ref, sh_ref, gt_ref, w_in_ref, cw_ref, cbias_ref, cg_ref, cb_ref,
                  w_co_ref, pw_ref, ps_ref, w_po_ref, w_o_ref, g_ref, b_ref,
                  o_ref, nconv_ref, npool_ref, conv_hist, conv_even, conv_odd, conv_wb, conv_out, pool_hist, work):
    b = pl.program_id(0)
    t = pl.program_id(1)
    tile = x_ref.shape[0]
    hist_words = HIST_PAD // 2
    tile_words = tile // 2

    @pl.when((b == 0) & (t == 0))
    def _():
        for j in range(N_LANE_TILES):
            for k in range(CONV_WIDTH):
                conv_wb[j, k] = jnp.broadcast_to(cw_ref[k:k + 1, _lane_tile(j)], (BF16_ROWS, LANES)).astype(BF16)

    @pl.when(t == 0)
    def _():
        conv_hist[:, 0:HIST_PAD, :] = jnp.zeros((N_LANE_TILES, HIST_PAD, LANES), F32)
        pool_hist[:, 0:HIST_PAD, :] = jnp.zeros((N_LANE_TILES, HIST_PAD, LANES), F32)
        conv_even[:, 0:hist_words, :] = jnp.zeros((N_LANE_TILES, hist_words, LANES), U32)
        conv_odd[:, 0:hist_words, :] = jnp.zeros((N_LANE_TILES, hist_words, LANES), U32)

    @pl.when(t > 0)
    def _():
        conv_hist[:, 0:HIST_PAD, :] = conv_hist[:, tile:tile + HIST_PAD, :]
        pool_hist[:, 0:HIST_PAD, :] = pool_hist[:, tile:tile + HIST_PAD, :]
        conv_even[:, 0:hist_words, :] = conv_even[:, tile_words:tile_words + hist_words, :]
        conv_odd[:, 0:hist_words, :] = conv_odd[:, tile_words:tile_words + hist_words, :]

    x = x_ref[...]
    h = (x * (1.0 + sc_ref[pl.ds(b, 1), :]) + sh_ref[pl.ds(b, 1), :]).astype(BF16)

    glu = _dot(h, w_in_ref[:, 0:D_MODEL])
    glu = glu * jax.nn.sigmoid(_dot(h, w_in_ref[:, D_MODEL:2 * D_MODEL]))
    for j in range(N_LANE_TILES):
        rows = glu[:, _lane_tile(j)]
        conv_hist[j, HIST_PAD:HIST_PAD + tile, :] = rows
        conv_even[j, hist_words:hist_words + tile_words, :] = _pack_rows(rows)
    for j in range(N_LANE_TILES):
        conv_odd[j, hist_words:hist_words + tile_words, :] = _pack_rows(
            conv_hist[j, HIST_PAD - 1:HIST_PAD - 1 + tile, :])

    u = _dot(h, w_in_ref[:, 2 * D_MODEL:3 * D_MODEL])
    for j in range(N_LANE_TILES):
        pool_hist[j, HIST_PAD:HIST_PAD + tile, :] = u[:, _lane_tile(j)]
    ga = _dot(h, w_in_ref[:, 3 * D_MODEL:4 * D_MODEL])
    gb = _dot(h, w_in_ref[:, 4 * D_MODEL:5 * D_MODEL])

    def conv_rows(c, carry):
        r0 = pl.multiple_of(c * CONV_ROWS, CONV_ROWS)
        q0 = hist_words + pl.multiple_of(c * (CONV_ROWS // 2), CONV_ROWS // 2)
        for j in range(N_LANE_TILES):
            conv_out[j, pl.ds(r0, CONV_ROWS), :] = _conv_chunk(conv_even, conv_odd, conv_wb, j, q0)
        return carry

    lax.fori_loop(0, tile // CONV_ROWS, conv_rows, 0)
    conv = jnp.concatenate([conv_out[j] + cbias_ref[:, _lane_tile(j)] for j in range(N_LANE_TILES)], axis=-1)

    pos1 = t * tile + 1 + lax.broadcasted_iota(jnp.int32, (POOL_MAX, LANES), 0)
    pooled = []
    for j in range(N_LANE_TILES):
        w = POOL_WINDOWS[j * LANES // POOL_GW]
        s = _window_sum(pool_hist, work, j, w, tile)
        head = s[0:POOL_MAX] / jnp.minimum(w, pos1).astype(F32)
        mean = jnp.concatenate([head, s[POOL_MAX:] * (1.0 / w)], axis=0)
        pooled.append(mean - u[:, _lane_tile(j)])
    pooled = jnp.concatenate(pooled, axis=-1)

    o_ref[...] = _mixer_tail(x, gt_ref[pl.ds(b, 1), :], conv, pooled, ga, gb, cg_ref, cb_ref,
                             w_co_ref, pw_ref, ps_ref, w_po_ref, w_o_ref, g_ref, b_ref)

    @pl.when((b == 0) & (t == 0))
    def _():
        nconv_ref[...] = jnp.zeros(nconv_ref.shape, F32)
        npool_ref[...] = jnp.zeros(npool_ref.shape, F32)

    @pl.when(t == pl.num_programs(1) - 1)
    def _():
        nb = nconv_ref.shape[1]
        mine = lax.broadcasted_iota(jnp.int32, (nb, LANES), 0) == b
        last = HIST_PAD + tile
        for hist, out_ref, n_rows in ((conv_hist, nconv_ref, CONV_HIST), (pool_hist, npool_ref, POOL_HIST)):
            for j in range(N_LANE_TILES):
                for r in range(n_rows):
                    src = last - n_rows + r
                    new = jnp.broadcast_to(hist[j, src:src + 1, :], (nb, LANES))
                    out_ref[r, :, _lane_tile(j)] = jnp.where(mine, new, out_ref[r, :, _lane_tile(j)])


def _mixer_weight_specs():
    return [
        _const_spec((CONV_WIDTH, D_MODEL)),
        _const_spec((1, D_MODEL)),
        _const_spec((1, D_MODEL)),
        _const_spec((1, D_MODEL)),
        _const_spec((D_MODEL, D_MODEL)),
        _const_spec((len(POOL_WINDOWS), POOL_GW, POOL_GW)),
        _const_spec((1, D_MODEL)),
        _const_spec((D_MODEL, D_MODEL)),
        _const_spec((D_MODEL, D_MODEL)),
        _const_spec((1, D_MODEL)),
        _const_spec((1, D_MODEL)),
    ]


def _mixer_prompt(x, mod, w_in, tail_w, *, tile):
    nb, nt, _ = x.shape
    assert tile >= HIST_PAD and tile % CONV_ROWS == 0 and nt % tile == 0
    prompt_block = (mod.shape[1] - nb) // nb
    mod_specs = [pl.BlockSpec((None, nb, D_MODEL), functools.partial(lambda b, t, i: (i, prompt_block, 0), i=i))
                 for i in (4, 3, 5)]
    hist = (N_LANE_TILES, HIST_PAD + tile, LANES)
    hist_packed = (N_LANE_TILES, (HIST_PAD + tile) // 2, LANES)
    return pl.pallas_call(
        _mixer_kernel,
        grid=(nb, nt // tile),
        in_specs=[pl.BlockSpec((None, tile, D_MODEL), lambda b, t: (b, t, 0))] + mod_specs
        + [_const_spec((D_MODEL, 5 * D_MODEL))] + _mixer_weight_specs(),
        out_specs=[
            pl.BlockSpec((None, tile, D_MODEL), lambda b, t: (b, t, 0)),
            pl.BlockSpec((CONV_HIST, nb, D_MODEL), lambda b, t: (0, 0, 0)),
            pl.BlockSpec((POOL_HIST, nb, D_MODEL), lambda b, t: (0, 0, 0)),
        ],
        out_shape=[
            jax.ShapeDtypeStruct(x.shape, F32),
            jax.ShapeDtypeStruct((CONV_HIST, nb, D_MODEL), F32),
            jax.ShapeDtypeStruct((POOL_HIST, nb, D_MODEL), F32),
        ],
        scratch_shapes=[
            pltpu.VMEM(hist, F32),
            pltpu.VMEM(hist_packed, U32),
            pltpu.VMEM(hist_packed, U32),
            pltpu.VMEM((N_LANE_TILES, CONV_WIDTH, BF16_ROWS, LANES), BF16),
            pltpu.VMEM((N_LANE_TILES, tile, LANES), F32),
            pltpu.VMEM(hist, F32),
            pltpu.VMEM((N_LANE_TILES, 2) + hist[1:], F32),
        ],
        compiler_params=pltpu.CompilerParams(
            dimension_semantics=("arbitrary", "arbitrary"), vmem_limit_bytes=VMEM_LIMIT),
        name="mixer",
    )(x, mod, mod, mod, w_in, *tail_w)


def _sample_proj_kernel(x_ref, sc_ref, sh_ref, w_in_ref, glu_ref, u_ref, ga_ref, gb_ref):
    h = (x_ref[...] * (1.0 + sc_ref[...]) + sh_ref[...]).astype(BF16)
    a = _dot(h, w_in_ref[:, 0:D_MODEL])
    glu_ref[...] = a * jax.nn.sigmoid(_dot(h, w_in_ref[:, D_MODEL:2 * D_MODEL]))
    u_ref[...] = _dot(h, w_in_ref[:, 2 * D_MODEL:3 * D_MODEL])
    ga_ref[...] = _dot(h, w_in_ref[:, 3 * D_MODEL:4 * D_MODEL])
    gb_ref[...] = _dot(h, w_in_ref[:, 4 * D_MODEL:5 * D_MODEL])


def _sample_state_kernel(sconv_ref, spool_ref, glu_ref, u_ref, cw_ref, cbias_ref,
                         nconv_ref, npool_ref, conv_ref, pooled_ref):
    glu = glu_ref[...]
    conv = cbias_ref[...] + glu * cw_ref[CONV_HIST:CONV_WIDTH, :]
    for k in range(CONV_HIST):
        conv = conv + sconv_ref[k] * cw_ref[k:k + 1, :]
    conv_ref[...] = conv
    nconv_ref[0:CONV_HIST - 1] = sconv_ref[1:CONV_HIST]
    nconv_ref[CONV_HIST - 1] = glu

    u = u_ref[...]
    w = jnp.left_shift(2, pl.program_id(0))
    s = u
    for back in range(1, POOL_MAX):
        s = s + jnp.where(back < w, spool_ref[POOL_HIST - back], 0.0)
    pooled_ref[...] = s / jnp.minimum(w, PAST_LEN + 1).astype(F32) - u
    npool_ref[0:POOL_HIST - 1] = spool_ref[1:POOL_HIST]
    npool_ref[POOL_HIST - 1] = u


def _sample_tail_kernel(x_ref, gt_ref, conv_ref, pooled_ref, ga_ref, gb_ref, cg_ref, cb_ref, w_co_ref,
                        pw_ref, ps_ref, w_po_ref, w_o_ref, g_ref, b_ref, o_ref):
    o_ref[...] = _mixer_tail(x_ref[...], gt_ref[...], conv_ref[...], pooled_ref[...], ga_ref[...], gb_ref[...],
                             cg_ref, cb_ref, w_co_ref, pw_ref, ps_ref, w_po_ref, w_o_ref, g_ref, b_ref)


def _mixer_sample(x, mod, state_conv, state_pool, w_in, tail_w):
    ns = x.shape[0]
    row_spec = pl.BlockSpec((ns, D_MODEL), lambda i: (0, 0))
    row_shape = jax.ShapeDtypeStruct((ns, D_MODEL), F32)

    def mod_spec(i):
        return pl.BlockSpec((None, ns, D_MODEL), lambda _: (i, 0, 0))

    glu, u, ga, gb = pl.pallas_call(
        _sample_proj_kernel,
        grid=(1,),
        in_specs=[row_spec, mod_spec(4), mod_spec(3), _const_spec((D_MODEL, 5 * D_MODEL))],
        out_specs=[row_spec] * 4,
        out_shape=[row_shape] * 4,
        compiler_params=pltpu.CompilerParams(dimension_semantics=("arbitrary",), vmem_limit_bytes=VMEM_LIMIT),
        name="sample_proj",
    )(x, mod, mod, w_in)

    conv_w, conv_b = tail_w[0], tail_w[1]
    tok_spec = pl.BlockSpec((ns, POOL_GW), lambda g: (0, g))
    nconv, npool, conv, pooled = pl.pallas_call(
        _sample_state_kernel,
        grid=(len(POOL_WINDOWS),),
        in_specs=[
            pl.BlockSpec((CONV_HIST, ns, POOL_GW), lambda g: (0, 0, g)),
            pl.BlockSpec((POOL_HIST, ns, POOL_GW), lambda g: (0, 0, g)),
            tok_spec, tok_spec,
            pl.BlockSpec((CONV_WIDTH, POOL_GW), lambda g: (0, g)),
            pl.BlockSpec((1, POOL_GW), lambda g: (0, g)),
        ],
        out_specs=[
            pl.BlockSpec((CONV_HIST, ns, POOL_GW), lambda g: (0, 0, g)),
            pl.BlockSpec((POOL_HIST, ns, POOL_GW), lambda g: (0, 0, g)),
            tok_spec, tok_spec,
        ],
        out_shape=[
            jax.ShapeDtypeStruct((CONV_HIST, ns, D_MODEL), F32),
            jax.ShapeDtypeStruct((POOL_HIST, ns, D_MODEL), F32),
            row_shape, row_shape,
        ],
        compiler_params=pltpu.CompilerParams(dimension_semantics=("arbitrary",), vmem_limit_bytes=VMEM_LIMIT),
        name="sample_state",
    )(state_conv, state_pool, glu, u, conv_w, conv_b)

    out = pl.pallas_call(
        _sample_tail_kernel,
        grid=(1,),
        in_specs=[row_spec, mod_spec(5), row_spec, row_spec, row_spec, row_spec] + _mixer_weight_specs()[2:],
        out_specs=row_spec,
        out_shape=row_shape,
        compiler_params=pltpu.CompilerParams(dimension_semantics=("arbitrary",), vmem_limit_bytes=VMEM_LIMIT),
        name="sample_tail",
    )(x, mod, conv, pooled, ga, gb, *tail_w[2:])
    return out, nconv, npool


def _time_major(state):
    return jnp.transpose(state, (0, 2, 1, 3))[0]


def _batch_major(state):
    return jnp.transpose(state[None], (0, 2, 1, 3))


def kernel(x_prompt, x_sample, state_conv, state_pool, c_prompt, c_sample, w_ada, b_ada, ffn1_w_in, ffn1_w_out, ln1_g, ln1_b, w_in, conv_w, conv_b, conv_ln_g, conv_ln_b, w_conv_out, pool_w, pool_scale, w_pool_out, w_out, ln2_g, ln2_b, ffn2_w_in, ffn2_w_out, ln3_g, ln3_b):
    nb, nt, d = x_prompt.shape
    ns, st, _ = x_sample.shape
    assert d == D_MODEL and st == 1 and w_ada.shape[0] == DEPTH
    assert nt % TOKEN_TILE == 0 and ns % nb == 0

    mod = _ada(jnp.concatenate([c_sample, c_prompt], axis=0), w_ada[0], b_ada[0])

    row = lambda v: v[0].reshape(1, D_MODEL)
    ffn1 = (ffn1_w_in.reshape(D_MODEL, 2 * D_FF), ffn1_w_out.reshape(D_FF, D_MODEL), row(ln1_g), row(ln1_b))
    ffn2 = (ffn2_w_in.reshape(D_MODEL, 2 * D_FF), ffn2_w_out.reshape(D_FF, D_MODEL), row(ln3_g), row(ln3_b))
    w_in_b = w_in[0].astype(BF16)
    tail_w = (conv_w[0], row(conv_b), row(conv_ln_g), row(conv_ln_b), w_conv_out[0].astype(BF16),
              pool_w[0].astype(BF16), row(pool_scale), w_pool_out[0].astype(BF16), w_out[0].astype(BF16),
              row(ln2_g), row(ln2_b))

    xp, xs = _ffn(x_prompt, x_sample.reshape(ns, D_MODEL), mod, (1, 0, 2), *ffn1, tile=TOKEN_TILE)
    xp, nconv_p, npool_p = _mixer_prompt(xp, mod, w_in_b, tail_w, tile=TOKEN_TILE)
    xs, nconv_s, npool_s = _mixer_sample(xs, mod, _time_major(state_conv), _time_major(state_pool),
                                         w_in_b, tail_w)
    xp, xs = _ffn(xp, xs, mod, (7, 6, 8), *ffn2, tile=TOKEN_TILE)

    return (xp, xs.reshape(ns, 1, D_MODEL), _batch_major(nconv_p), _batch_major(npool_p),
            _batch_major(nconv_s), _batch_major(npool_s))
```

```python
import functools

import jax
import jax.numpy as jnp
from jax import lax
from jax.experimental import pallas as pl
from jax.experimental.pallas import tpu as pltpu

D_MODEL = 1024
CONV_WIDTH = 31
CONV_HIST = CONV_WIDTH - 1
POOL_WINDOWS = (2, 4, 8, 16)
POOL_GW = D_MODEL // len(POOL_WINDOWS)
POOL_MAX = 16
POOL_HIST = POOL_MAX - 1
D_FF = ((8 * D_MODEL // 3 + 127) // 128) * 128
N_MOD = 9
ADA_GROUP = 3
DEPTH = 1
DN_ALPHA = (2.0 * DEPTH) ** 0.25
FFN_RES = 0.5
LN_EPS = 1e-5
PAST_LEN = 16384

SUBLANES = 8
LANES = 128
BF16_ROWS = 2 * SUBLANES
N_LANE_TILES = D_MODEL // LANES
FF_CHUNK = 256
N_FF_CHUNKS = D_FF // FF_CHUNK
FFN_STAGE_SLOTS = 2
TOKEN_TILE = 512
FFN_TILE = 512
FFN_LN_ROWS = 64
MIX_LN_ROWS = 64
HIST_PAD = 32
CONV_ROWS = 64
VMEM_LIMIT = 56 * 1024 * 1024

BF16 = jnp.bfloat16
F32 = jnp.float32
U32 = jnp.uint32

assert all(w & (w - 1) == 0 for w in POOL_WINDOWS) and POOL_GW == 2 * LANES
assert HIST_PAD >= CONV_HIST and HIST_PAD % BF16_ROWS == 0


def _const_spec(shape):
    n = len(shape)
    return pl.BlockSpec(shape, lambda *_: (0,) * n, pipeline_mode=pl.Buffered(1))


def _dot(a, b):
    return jnp.dot(a, b, preferred_element_type=F32)


def _layer_norm(z, g, b):
    mu = jnp.mean(z, axis=-1, keepdims=True)
    zc = z - mu
    var = jnp.mean(zc * zc, axis=-1, keepdims=True)
    return zc * lax.rsqrt(var + LN_EPS) * g + b


def _lane_tile(j):
    return slice(j * LANES, (j + 1) * LANES)


def _ada_kernel(cs_ref, cp_ref, w_ref, b_ref, o_ref):
    c = jnp.concatenate([cs_ref[...], cp_ref[...]], axis=0)
    s = (c * jax.nn.sigmoid(c)).astype(BF16)
    for q in range(ADA_GROUP):
        cols = slice(q * D_MODEL, (q + 1) * D_MODEL)
        o_ref[q] = _dot(s, w_ref[:, cols].astype(BF16)) + b_ref[:, cols]


def _ada(c_sample, c_prompt, w_ada, b_ada):
    ns, nb = c_sample.shape[0], c_prompt.shape[0]
    return pl.pallas_call(
        _ada_kernel,
        grid=(N_MOD // ADA_GROUP,),
        in_specs=[
            pl.BlockSpec((ns, D_MODEL), lambda i: (0, 0)),
            pl.BlockSpec((nb, D_MODEL), lambda i: (0, 0)),
            pl.BlockSpec((D_MODEL, ADA_GROUP * D_MODEL), lambda i: (0, i)),
            pl.BlockSpec((1, ADA_GROUP * D_MODEL), lambda i: (0, i)),
        ],
        out_specs=pl.BlockSpec((ADA_GROUP, ns + nb, D_MODEL), lambda i: (i, 0, 0)),
        out_shape=jax.ShapeDtypeStruct((N_MOD, ns + nb, D_MODEL), F32),
        compiler_params=pltpu.CompilerParams(dimension_semantics=("arbitrary",), vmem_limit_bytes=VMEM_LIMIT),
        name="ada",
    )(c_sample, c_prompt, w_ada, b_ada.reshape(1, N_MOD * D_MODEL))


def _ffn_chunk_copies(c, slot, w_in_hbm, w_out_hbm, stage_in, stage_out, sems):
    c0 = c * FF_CHUNK
    return (
        pltpu.make_async_copy(w_in_hbm.at[:, pl.ds(c0, FF_CHUNK)], stage_in.at[slot, 0], sems.at[slot, 0]),
        pltpu.make_async_copy(w_in_hbm.at[:, pl.ds(D_FF + c0, FF_CHUNK)], stage_in.at[slot, 1], sems.at[slot, 1]),
        pltpu.make_async_copy(w_out_hbm.at[pl.ds(c0, FF_CHUNK), :], stage_out.at[slot], sems.at[slot, 2]),
    )


def _ffn_tile(x, sc, sh, gt, w_in_bf, w_out_bf, before_chunk, after_gate):
    h = (x * (1.0 + sc) + sh).astype(BF16)
    acc = jnp.zeros(x.shape, F32)
    for c in range(N_FF_CHUNKS):
        before_chunk(c)
        c0 = c * FF_CHUNK
        gate = _dot(h, w_in_bf[:, c0:c0 + FF_CHUNK])
        after_gate(c, gate)
        up = _dot(h, w_in_bf[:, D_FF + c0:D_FF + c0 + FF_CHUNK])
        act = (gate * jax.nn.sigmoid(gate) * up).astype(BF16)
        acc = acc + _dot(act, w_out_bf[c0:c0 + FF_CHUNK, :])
    return DN_ALPHA * x + FFN_RES * gt * acc


def _zero_from(v):
    bits = pltpu.bitcast(v, U32)
    return pltpu.bitcast(jnp.right_shift(jnp.right_shift(bits, 16), 16), F32)


def _ffn_kernel(tiles_per_seq, n_prompt_steps,
                x_ref, xs_ref, sc_ref, sh_ref, gt_ref, scs_ref, shs_ref, gts_ref, w_in_hbm, w_out_hbm, g_ref, b_ref,
                o_ref, os_ref, w_in_bf, w_out_bf, stage_in, stage_out, z_prev, sems):
    s = pl.program_id(0)
    tile = x_ref.shape[0]
    copies = functools.partial(_ffn_chunk_copies, w_in_hbm=w_in_hbm, w_out_hbm=w_out_hbm,
                               stage_in=stage_in, stage_out=stage_out, sems=sems)

    def load_chunk(c):
        slot = c % FFN_STAGE_SLOTS
        c0 = c * FF_CHUNK
        for copy in copies(c, slot):
            copy.wait()
        w_in_bf[:, c0:c0 + FF_CHUNK] = stage_in[slot, 0].astype(BF16)
        w_in_bf[:, D_FF + c0:D_FF + c0 + FF_CHUNK] = stage_in[slot, 1].astype(BF16)
        w_out_bf[c0:c0 + FF_CHUNK, :] = stage_out[slot].astype(BF16)
        if c + FFN_STAGE_SLOTS < N_FF_CHUNKS:
            for copy in copies(c + FFN_STAGE_SLOTS, slot):
                copy.start()

    def norm_prev_piece(c, gate):
        r0 = (c - 1) * FFN_LN_ROWS
        if c < 1 or r0 >= tile:
            return
        z = z_prev[r0:r0 + FFN_LN_ROWS, :]
        tie = jnp.concatenate([_zero_from(gate[0:SUBLANES, 0:LANES])] * (FFN_LN_ROWS // SUBLANES), axis=0)
        z = jnp.concatenate([z[:, 0:LANES] + tie, z[:, LANES:]], axis=-1)
        o_ref[r0:r0 + FFN_LN_ROWS, :] = _layer_norm(z, g_ref[...], b_ref[...])

    def nothing(*_):
        return None

    def prompt_tile(before_chunk, after_gate):
        seq = s // tiles_per_seq
        z = _ffn_tile(x_ref[...], sc_ref[pl.ds(seq, 1), :], sh_ref[pl.ds(seq, 1), :], gt_ref[pl.ds(seq, 1), :],
                      w_in_bf, w_out_bf, before_chunk, after_gate)
        z_prev[...] = z

    @pl.when(s == 0)
    def _():
        for c in range(FFN_STAGE_SLOTS):
            for copy in copies(c, c):
                copy.start()
        prompt_tile(load_chunk, nothing)

    @pl.when((s > 0) & (s < n_prompt_steps))
    def _():
        prompt_tile(nothing, norm_prev_piece)

    @pl.when(s == n_prompt_steps)
    def _():
        z = _ffn_tile(xs_ref[...], scs_ref[...], shs_ref[...], gts_ref[...], w_in_bf, w_out_bf,
                      nothing, norm_prev_piece)
        os_ref[...] = _layer_norm(z, g_ref[...], b_ref[...])


def _ffn(x, xs, mod, mod_idx, w_in, w_out, ln_g, ln_b, *, tile):
    nb, nt, _ = x.shape
    ns = xs.shape[0]
    assert tile % FFN_LN_ROWS == 0 and tile // FFN_LN_ROWS < N_FF_CHUNKS
    tiles_per_seq = nt // tile
    n_prompt_steps = nb * tiles_per_seq
    prompt_block = ns // nb

    def tile_index(s):
        p = jnp.minimum(s, n_prompt_steps - 1)
        return (p // tiles_per_seq, p % tiles_per_seq, 0)

    def prev_tile_index(s):
        return tile_index(jnp.maximum(s - 1, 0))

    tile_spec = pl.BlockSpec((None, tile, D_MODEL), tile_index)
    prev_tile_spec = pl.BlockSpec((None, tile, D_MODEL), prev_tile_index)
    rows_spec = pl.BlockSpec((ns, D_MODEL), lambda s: (0, 0))
    prompt_mod = [pl.BlockSpec((None, nb, D_MODEL), functools.partial(lambda s, i: (i, prompt_block, 0), i=i))
                  for i in mod_idx]
    sample_mod = [pl.BlockSpec((None, ns, D_MODEL), functools.partial(lambda s, i: (i, 0, 0), i=i))
                  for i in mod_idx]
    return pl.pallas_call(
        functools.partial(_ffn_kernel, tiles_per_seq, n_prompt_steps),
        grid=(n_prompt_steps + 1,),
        in_specs=[tile_spec, rows_spec] + prompt_mod + sample_mod + [
            pl.BlockSpec(memory_space=pl.ANY),
            pl.BlockSpec(memory_space=pl.ANY),
            _const_spec((1, D_MODEL)),
            _const_spec((1, D_MODEL)),
        ],
        out_specs=[prev_tile_spec, rows_spec],
        out_shape=[jax.ShapeDtypeStruct(x.shape, F32), jax.ShapeDtypeStruct(xs.shape, F32)],
        scratch_shapes=[
            pltpu.VMEM((D_MODEL, 2 * D_FF), BF16),
            pltpu.VMEM((D_FF, D_MODEL), BF16),
            pltpu.VMEM((FFN_STAGE_SLOTS, 2, D_MODEL, FF_CHUNK), F32),
            pltpu.VMEM((FFN_STAGE_SLOTS, FF_CHUNK, D_MODEL), F32),
            pltpu.VMEM((tile, D_MODEL), F32),
            pltpu.SemaphoreType.DMA((FFN_STAGE_SLOTS, 3)),
        ],
        compiler_params=pltpu.CompilerParams(dimension_semantics=("arbitrary",), vmem_limit_bytes=VMEM_LIMIT),
        name="ffn",
    )(x, xs, mod, mod, mod, mod, mod, mod, w_in, w_out, ln_g, ln_b)


def _mixer_tail(x, gate_rows, conv, pooled, ga, gb, cg_ref, cb_ref, w_co_ref, pw_ref, ps_ref,
                w_po_ref, w_o_ref):
    ya_in = _layer_norm(conv, cg_ref[...], cb_ref[...])
    ya_in = (ya_in * jax.nn.sigmoid(ya_in)).astype(BF16)
    merged = jax.nn.sigmoid(ga) * _dot(ya_in, w_co_ref[...])
    pooled = pooled.astype(BF16)
    mixed = jnp.concatenate(
        [_dot(pooled[:, g * POOL_GW:(g + 1) * POOL_GW], pw_ref[g]) for g in range(len(POOL_WINDOWS))],
        axis=-1)
    yb = _dot((mixed * ps_ref[...]).astype(BF16), w_po_ref[...])
    merged = (merged + jax.nn.sigmoid(gb) * yb).astype(BF16)
    return DN_ALPHA * x + gate_rows * _dot(merged, w_o_ref[...])


def _window_sum(hist, work, j, w, tile):
    end = HIST_PAD + tile
    lo = {w: HIST_PAD}
    v = w
    while v > 2:
        lo[v // 2] = (lo[v] - v // 2) // SUBLANES * SUBLANES
        v //= 2
    s = hist[j, lo[2]:end, :] + hist[j, lo[2] - 1:end - 1, :]
    v, slot = 2, 0
    while v < w:
        work[j, slot, lo[v]:end, :] = s
        s = work[j, slot, lo[2 * v]:end, :] + work[j, slot, lo[2 * v] - v:end - v, :]
        v, slot = 2 * v, 1 - slot
    return s


def _pack_rows(rows_f32):
    return pltpu.bitcast(rows_f32.astype(BF16), U32)


def _conv_chunk(conv_even, conv_odd, conv_wb, j, q0):
    n = CONV_ROWS // 2
    taps, weights = [], []
    for k in range(CONV_WIDTH):
        back = CONV_HIST - k
        if back % 2 == 0:
            words = conv_even[j, pl.ds(q0 - back // 2, n), :]
        else:
            words = conv_odd[j, pl.ds(q0 - (back - 1) // 2, n), :]
        taps.append(pltpu.bitcast(words, BF16))
        weights.append(jnp.concatenate([conv_wb[j, k]] * (CONV_ROWS // BF16_ROWS), axis=0))
    taps = jnp.stack(taps, axis=0).astype(F32)
    weights = jnp.stack(weights, axis=0).astype(F32)
    return jnp.sum(taps * weights, axis=0)


def _mixer_kernel(tiles_per_seq, n_tiles,
                  x_ref, sc_ref, sh_ref, gt_ref, w_in_ref, cw_ref, cbias_ref, cg_ref, cb_ref,
                  w_co_ref, pw_ref, ps_ref, w_po_ref, w_o_ref, g_ref, b_ref,
                  o_ref, nconv_ref, npool_ref,
                  conv_hist, conv_even, conv_odd, conv_wb, conv_out, pool_hist, work, z_prev):
    s = pl.program_id(0)

    @pl.when(s == n_tiles)
    def _():
        o_ref[...] = _layer_norm(z_prev[...], g_ref[...], b_ref[...])

    @pl.when(s < n_tiles)
    def _():
        _mixer_tile(s // tiles_per_seq, s % tiles_per_seq, tiles_per_seq, s == 0,
                    x_ref, sc_ref, sh_ref, gt_ref, w_in_ref, cw_ref, cbias_ref, cg_ref, cb_ref,
                    w_co_ref, pw_ref, ps_ref, w_po_ref, w_o_ref, g_ref, b_ref, o_ref, nconv_ref, npool_ref,
                    conv_hist, conv_even, conv_odd, conv_wb, conv_out, pool_hist, work, z_prev)


def _mixer_tile(b, t, tiles_per_seq, first_step,
                x_ref, sc_ref, sh_ref, gt_ref, w_in_ref, cw_ref, cbias_ref, cg_ref, cb_ref,
                w_co_ref, pw_ref, ps_ref, w_po_ref, w_o_ref, g_ref, b_ref, o_ref, nconv_ref, npool_ref,
                conv_hist, conv_even, conv_odd, conv_wb, conv_out, pool_hist, work, z_prev):
    tile = x_ref.shape[0]
    hist_words = HIST_PAD // 2
    tile_words = tile // 2

    @pl.when(first_step)
    def _():
        z_prev[...] = jnp.zeros(z_prev.shape, F32)
        for j in range(N_LANE_TILES):
            for k in range(CONV_WIDTH):
                conv_wb[j, k] = jnp.broadcast_to(cw_ref[k:k + 1, _lane_tile(j)], (BF16_ROWS, LANES)).astype(BF16)

    @pl.when(t == 0)
    def _():
        conv_hist[:, 0:HIST_PAD, :] = jnp.zeros((N_LANE_TILES, HIST_PAD, LANES), F32)
        pool_hist[:, 0:HIST_PAD, :] = jnp.zeros((N_LANE_TILES, HIST_PAD, LANES), F32)
        conv_even[:, 0:hist_words, :] = jnp.zeros((N_LANE_TILES, hist_words, LANES), U32)
        conv_odd[:, 0:hist_words, :] = jnp.zeros((N_LANE_TILES, hist_words, LANES), U32)

    @pl.when(t > 0)
    def _():
        conv_hist[:, 0:HIST_PAD, :] = conv_hist[:, tile:tile + HIST_PAD, :]
        pool_hist[:, 0:HIST_PAD, :] = pool_hist[:, tile:tile + HIST_PAD, :]
        conv_even[:, 0:hist_words, :] = conv_even[:, tile_words:tile_words + hist_words, :]
        conv_odd[:, 0:hist_words, :] = conv_odd[:, tile_words:tile_words + hist_words, :]

    def norm_prev_piece(p, after):
        r0 = p * MIX_LN_ROWS
        z = z_prev[r0:r0 + MIX_LN_ROWS, :]
        tie = jnp.concatenate([_zero_from(after[0:SUBLANES, 0:LANES])] * (MIX_LN_ROWS // SUBLANES), axis=0)
        z = jnp.concatenate([z[:, 0:LANES] + tie, z[:, LANES:]], axis=-1)
        o_ref[r0:r0 + MIX_LN_ROWS, :] = _layer_norm(z, g_ref[...], b_ref[...])

    x = x_ref[...]
    h = (x * (1.0 + sc_ref[pl.ds(b, 1), :]) + sh_ref[pl.ds(b, 1), :]).astype(BF16)

    parts = ([], [])
    for c in range(D_MODEL // POOL_GW):
        for half in range(2):
            c0 = half * D_MODEL + c * POOL_GW
            d = _dot(h, w_in_ref[:, c0:c0 + POOL_GW])
            parts[half].append(d)
            norm_prev_piece(2 * c + half, d)
    glu = jnp.concatenate(parts[0], axis=-1) * jax.nn.sigmoid(jnp.concatenate(parts[1], axis=-1))
    for j in range(N_LANE_TILES):
        rows = glu[:, _lane_tile(j)]
        conv_hist[j, HIST_PAD:HIST_PAD + tile, :] = rows
        conv_even[j, hist_words:hist_words + tile_words, :] = _pack_rows(rows)
    for j in range(N_LANE_TILES):
        conv_odd[j, hist_words:hist_words + tile_words, :] = _pack_rows(
            conv_hist[j, HIST_PAD - 1:HIST_PAD - 1 + tile, :])

    u = _dot(h, w_in_ref[:, 2 * D_MODEL:3 * D_MODEL])
    for j in range(N_LANE_TILES):
        pool_hist[j, HIST_PAD:HIST_PAD + tile, :] = u[:, _lane_tile(j)]
    ga = _dot(h, w_in_ref[:, 3 * D_MODEL:4 * D_MODEL])
    gb = _dot(h, w_in_ref[:, 4 * D_MODEL:5 * D_MODEL])

    def conv_rows(c, carry):
        r0 = pl.multiple_of(c * CONV_ROWS, CONV_ROWS)
        q0 = hist_words + pl.multiple_of(c * (CONV_ROWS // 2), CONV_ROWS // 2)
        for j in range(N_LANE_TILES):
            conv_out[j, pl.ds(r0, CONV_ROWS), :] = _conv_chunk(conv_even, conv_odd, conv_wb, j, q0)
        return carry

    lax.fori_loop(0, tile // CONV_ROWS, conv_rows, 0)
    conv = jnp.concatenate([conv_out[j] + cbias_ref[:, _lane_tile(j)] for j in range(N_LANE_TILES)], axis=-1)

    pos1 = t * tile + 1 + lax.broadcasted_iota(jnp.int32, (POOL_MAX, LANES), 0)
    pooled = []
    for j in range(N_LANE_TILES):
        w = POOL_WINDOWS[j * LANES // POOL_GW]
        s = _window_sum(pool_hist, work, j, w, tile)
        head = s[0:POOL_MAX] / jnp.minimum(w, pos1).astype(F32)
        mean = jnp.concatenate([head, s[POOL_MAX:] * (1.0 / w)], axis=0)
        pooled.append(mean - u[:, _lane_tile(j)])
    pooled = jnp.concatenate(pooled, axis=-1)

    z_prev[...] = _mixer_tail(x, gt_ref[pl.ds(b, 1), :], conv, pooled, ga, gb, cg_ref, cb_ref,
                              w_co_ref, pw_ref, ps_ref, w_po_ref, w_o_ref)

    @pl.when(first_step)
    def _():
        nconv_ref[...] = jnp.zeros(nconv_ref.shape, F32)
        npool_ref[...] = jnp.zeros(npool_ref.shape, F32)

    @pl.when(t == tiles_per_seq - 1)
    def _():
        nb = nconv_ref.shape[1]
        mine = lax.broadcasted_iota(jnp.int32, (nb, LANES), 0) == b
        last = HIST_PAD + tile
        for hist, out_ref, n_rows in ((conv_hist, nconv_ref, CONV_HIST), (pool_hist, npool_ref, POOL_HIST)):
            for j in range(N_LANE_TILES):
                for r in range(n_rows):
                    src = last - n_rows + r
                    new = jnp.broadcast_to(hist[j, src:src + 1, :], (nb, LANES))
                    out_ref[r, :, _lane_tile(j)] = jnp.where(mine, new, out_ref[r, :, _lane_tile(j)])


def _mixer_weight_specs():
    return [
        _const_spec((CONV_WIDTH, D_MODEL)),
        _const_spec((1, D_MODEL)),
        _const_spec((1, D_MODEL)),
        _const_spec((1, D_MODEL)),
        _const_spec((D_MODEL, D_MODEL)),
        _const_spec((len(POOL_WINDOWS), POOL_GW, POOL_GW)),
        _const_spec((1, D_MODEL)),
        _const_spec((D_MODEL, D_MODEL)),
        _const_spec((D_MODEL, D_MODEL)),
        _const_spec((1, D_MODEL)),
        _const_spec((1, D_MODEL)),
    ]


def _mixer_prompt(x, mod, w_in, tail_w, *, tile):
    nb, nt, _ = x.shape
    assert tile >= HIST_PAD and tile % CONV_ROWS == 0 and nt % tile == 0
    assert tile % MIX_LN_ROWS == 0 and tile // MIX_LN_ROWS == 2 * (D_MODEL // POOL_GW)
    prompt_block = (mod.shape[1] - nb) // nb
    tiles_per_seq = nt // tile
    n_tiles = nb * tiles_per_seq

    def tile_index(s):
        p = jnp.clip(s, 0, n_tiles - 1)
        return (p // tiles_per_seq, p % tiles_per_seq, 0)

    mod_specs = [pl.BlockSpec((None, nb, D_MODEL), functools.partial(lambda s, i: (i, prompt_block, 0), i=i))
                 for i in (4, 3, 5)]
    hist = (N_LANE_TILES, HIST_PAD + tile, LANES)
    hist_packed = (N_LANE_TILES, (HIST_PAD + tile) // 2, LANES)
    return pl.pallas_call(
        functools.partial(_mixer_kernel, tiles_per_seq, n_tiles),
        grid=(n_tiles + 1,),
        in_specs=[pl.BlockSpec((None, tile, D_MODEL), tile_index)] + mod_specs
        + [_const_spec((D_MODEL, 5 * D_MODEL))] + _mixer_weight_specs(),
        out_specs=[
            pl.BlockSpec((None, tile, D_MODEL), lambda s: tile_index(s - 1)),
            pl.BlockSpec((CONV_HIST, nb, D_MODEL), lambda s: (0, 0, 0)),
            pl.BlockSpec((POOL_HIST, nb, D_MODEL), lambda s: (0, 0, 0)),
        ],
        out_shape=[
            jax.ShapeDtypeStruct(x.shape, F32),
            jax.ShapeDtypeStruct((CONV_HIST, nb, D_MODEL), F32),
            jax.ShapeDtypeStruct((POOL_HIST, nb, D_MODEL), F32),
        ],
        scratch_shapes=[
            pltpu.VMEM(hist, F32),
            pltpu.VMEM(hist_packed, U32),
            pltpu.VMEM(hist_packed, U32),
            pltpu.VMEM((N_LANE_TILES, CONV_WIDTH, BF16_ROWS, LANES), BF16),
            pltpu.VMEM((N_LANE_TILES, tile, LANES), F32),
            pltpu.VMEM(hist, F32),
            pltpu.VMEM((N_LANE_TILES, 2) + hist[1:], F32),
            pltpu.VMEM((tile, D_MODEL), F32),
        ],
        compiler_params=pltpu.CompilerParams(dimension_semantics=("arbitrary",), vmem_limit_bytes=VMEM_LIMIT),
        name="mixer",
    )(x, mod, mod, mod, w_in, *tail_w)


def _sample_proj_kernel(x_ref, sc_ref, sh_ref, w_in_ref, glu_ref, u_ref, ga_ref, gb_ref):
    h = (x_ref[...] * (1.0 + sc_ref[...]) + sh_ref[...]).astype(BF16)
    a = _dot(h, w_in_ref[:, 0:D_MODEL])
    glu_ref[...] = a * jax.nn.sigmoid(_dot(h, w_in_ref[:, D_MODEL:2 * D_MODEL]))
    u_ref[...] = _dot(h, w_in_ref[:, 2 * D_MODEL:3 * D_MODEL])
    ga_ref[...] = _dot(h, w_in_ref[:, 3 * D_MODEL:4 * D_MODEL])
    gb_ref[...] = _dot(h, w_in_ref[:, 4 * D_MODEL:5 * D_MODEL])


def _sample_state_kernel(sconv_ref, spool_ref, glu_ref, u_ref, cw_ref, cbias_ref,
                         nconv_ref, npool_ref, conv_ref, pooled_ref):
    glu = glu_ref[...]
    conv = cbias_ref[...] + glu * cw_ref[CONV_HIST:CONV_WIDTH, :]
    for k in range(CONV_HIST):
        conv = conv + sconv_ref[k] * cw_ref[k:k + 1, :]
    conv_ref[...] = conv
    nconv_ref[0:CONV_HIST - 1] = sconv_ref[1:CONV_HIST]
    nconv_ref[CONV_HIST - 1] = glu

    u = u_ref[...]
    w = jnp.left_shift(2, pl.program_id(0))
    s = u
    for back in range(1, POOL_MAX):
        s = s + jnp.where(back < w, spool_ref[POOL_HIST - back], 0.0)
    pooled_ref[...] = s / jnp.minimum(w, PAST_LEN + 1).astype(F32) - u
    npool_ref[0:POOL_HIST - 1] = spool_ref[1:POOL_HIST]
    npool_ref[POOL_HIST - 1] = u


def _sample_tail_kernel(x_ref, gt_ref, conv_ref, pooled_ref, ga_ref, gb_ref, cg_ref, cb_ref, w_co_ref,
                        pw_ref, ps_ref, w_po_ref, w_o_ref, g_ref, b_ref, o_ref):
    z = _mixer_tail(x_ref[...], gt_ref[...], conv_ref[...], pooled_ref[...], ga_ref[...], gb_ref[...],
                    cg_ref, cb_ref, w_co_ref, pw_ref, ps_ref, w_po_ref, w_o_ref)
    o_ref[...] = _layer_norm(z, g_ref[...], b_ref[...])


def _mixer_sample(x, mod, state_conv, state_pool, w_in, tail_w):
    ns = x.shape[0]
    row_spec = pl.BlockSpec((ns, D_MODEL), lambda i: (0, 0))
    row_shape = jax.ShapeDtypeStruct((ns, D_MODEL), F32)

    def mod_spec(i):
        return pl.BlockSpec((None, ns, D_MODEL), lambda _: (i, 0, 0))

    glu, u, ga, gb = pl.pallas_call(
        _sample_proj_kernel,
        grid=(1,),
        in_specs=[row_spec, mod_spec(4), mod_spec(3), _const_spec((D_MODEL, 5 * D_MODEL))],
        out_specs=[row_spec] * 4,
        out_shape=[row_shape] * 4,
        compiler_params=pltpu.CompilerParams(dimension_semantics=("arbitrary",), vmem_limit_bytes=VMEM_LIMIT),
        name="sample_proj",
    )(x, mod, mod, w_in)

    conv_w, conv_b = tail_w[0], tail_w[1]
    tok_spec = pl.BlockSpec((ns, POOL_GW), lambda g: (0, g))
    nconv, npool, conv, pooled = pl.pallas_call(
        _sample_state_kernel,
        grid=(len(POOL_WINDOWS),),
        in_specs=[
            pl.BlockSpec((CONV_HIST, ns, POOL_GW), lambda g: (0, 0, g)),
            pl.BlockSpec((POOL_HIST, ns, POOL_GW), lambda g: (0, 0, g)),
            tok_spec, tok_spec,
            pl.BlockSpec((CONV_WIDTH, POOL_GW), lambda g: (0, g)),
            pl.BlockSpec((1, POOL_GW), lambda g: (0, g)),
        ],
        out_specs=[
            pl.BlockSpec((CONV_HIST, ns, POOL_GW), lambda g: (0, 0, g)),
            pl.BlockSpec((POOL_HIST, ns, POOL_GW), lambda g: (0, 0, g)),
            tok_spec, tok_spec,
        ],
        out_shape=[
            jax.ShapeDtypeStruct((CONV_HIST, ns, D_MODEL), F32),
            jax.ShapeDtypeStruct((POOL_HIST, ns, D_MODEL), F32),
            row_shape, row_shape,
        ],
        compiler_params=pltpu.CompilerParams(dimension_semantics=("arbitrary",), vmem_limit_bytes=VMEM_LIMIT),
        name="sample_state",
    )(state_conv, state_pool, glu, u, conv_w, conv_b)

    out = pl.pallas_call(
        _sample_tail_kernel,
        grid=(1,),
        in_specs=[row_spec, mod_spec(5), row_spec, row_spec, row_spec, row_spec] + _mixer_weight_specs()[2:],
        out_specs=row_spec,
        out_shape=row_shape,
        compiler_params=pltpu.CompilerParams(dimension_semantics=("arbitrary",), vmem_limit_bytes=VMEM_LIMIT),
        name="sample_tail",
    )(x, mod, conv, pooled, ga, gb, *tail_w[2:])
    return out, nconv, npool


def _time_major(state):
    return jnp.transpose(state, (0, 2, 1, 3))[0]


def _batch_major(state):
    return jnp.transpose(state[None], (0, 2, 1, 3))


def kernel(x_prompt, x_sample, state_conv, state_pool, c_prompt, c_sample, w_ada, b_ada, ffn1_w_in, ffn1_w_out, ln1_g, ln1_b, w_in, conv_w, conv_b, conv_ln_g, conv_ln_b, w_conv_out, pool_w, pool_scale, w_pool_out, w_out, ln2_g, ln2_b, ffn2_w_in, ffn2_w_out, ln3_g, ln3_b):
    nb, nt, d = x_prompt.shape
    ns, st, _ = x_sample.shape
    assert d == D_MODEL and st == 1 and w_ada.shape[0] == DEPTH
    assert nt % TOKEN_TILE == 0 and nt % FFN_TILE == 0 and ns % nb == 0 and N_MOD % ADA_GROUP == 0

    mod = _ada(c_sample, c_prompt, w_ada.reshape(D_MODEL, N_MOD * D_MODEL), b_ada[0])

    row = lambda v: v[0].reshape(1, D_MODEL)
    ffn1 = (ffn1_w_in.reshape(D_MODEL, 2 * D_FF), ffn1_w_out.reshape(D_FF, D_MODEL), row(ln1_g), row(ln1_b))
    ffn2 = (ffn2_w_in.reshape(D_MODEL, 2 * D_FF), ffn2_w_out.reshape(D_FF, D_MODEL), row(ln3_g), row(ln3_b))
    w_in_b = w_in[0].astype(BF16)
    tail_w = (conv_w[0], row(conv_b), row(conv_ln_g), row(conv_ln_b), w_conv_out[0].astype(BF16),
              pool_w[0].astype(BF16), row(pool_scale), w_pool_out[0].astype(BF16), w_out[0].astype(BF16),
              row(ln2_g), row(ln2_b))

    xp, xs = _ffn(x_prompt, x_sample.reshape(ns, D_MODEL), mod, (1, 0, 2), *ffn1, tile=FFN_TILE)
    xp, nconv_p, npool_p = _mixer_prompt(xp, mod, w_in_b, tail_w, tile=TOKEN_TILE)
    xs, nconv_s, npool_s = _mixer_sample(xs, mod, _time_major(state_conv), _time_major(state_pool),
                                         w_in_b, tail_w)
    xp, xs = _ffn(xp, xs, mod, (7, 6, 8), *ffn2, tile=FFN_TILE)

    return (xp, xs.reshape(ns, 1, D_MODEL), _batch_major(nconv_p), _batch_major(npool_p),
            _batch_major(nconv_s), _batch_major(npool_s))
```

```python
import functools

import jax
import jax.numpy as jnp
from jax import lax
from jax.experimental import pallas as pl
from jax.experimental.pallas import tpu as pltpu

D_MODEL = 1024
CONV_WIDTH = 31
CONV_HIST = CONV_WIDTH - 1
POOL_WINDOWS = (2, 4, 8, 16)
POOL_GW = D_MODEL // len(POOL_WINDOWS)
POOL_MAX = 16
POOL_HIST = POOL_MAX - 1
D_FF = ((8 * D_MODEL // 3 + 127) // 128) * 128
N_MOD = 9
ADA_GROUP = 3
DEPTH = 1
DN_ALPHA = (2.0 * DEPTH) ** 0.25
FFN_RES = 0.5
LN_EPS = 1e-5
PAST_LEN = 16384

SUBLANES = 8
LANES = 128
BF16_ROWS = 2 * SUBLANES
N_LANE_TILES = D_MODEL // LANES
FF_CHUNK = 256
N_FF_CHUNKS = D_FF // FF_CHUNK
FFN_STAGE_SLOTS = 2
TOKEN_TILE = 512
FFN_TILE = 512
FFN_LN_ROWS = 64
MIX_LN_ROWS = 64
HIST_PAD = 32
CONV_ROWS = 64
VMEM_LIMIT = 56 * 1024 * 1024

BF16 = jnp.bfloat16
F32 = jnp.float32
U32 = jnp.uint32

assert all(w & (w - 1) == 0 for w in POOL_WINDOWS) and POOL_GW == 2 * LANES
assert HIST_PAD >= CONV_HIST and HIST_PAD % BF16_ROWS == 0


def _const_spec(shape):
    n = len(shape)
    return pl.BlockSpec(shape, lambda *_: (0,) * n, pipeline_mode=pl.Buffered(1))


def _dot(a, b):
    return jnp.dot(a, b, preferred_element_type=F32)


def _layer_norm(z, g, b):
    mu = jnp.mean(z, axis=-1, keepdims=True)
    zc = z - mu
    var = jnp.mean(zc * zc, axis=-1, keepdims=True)
    return zc * lax.rsqrt(var + LN_EPS) * g + b


def _lane_tile(j):
    return slice(j * LANES, (j + 1) * LANES)


def _ada_kernel(cs_ref, cp_ref, w_ref, b_ref, o_ref):
    c = jnp.concatenate([cs_ref[...], cp_ref[...]], axis=0)
    s = (c * jax.nn.sigmoid(c)).astype(BF16)
    for q in range(ADA_GROUP):
        cols = slice(q * D_MODEL, (q + 1) * D_MODEL)
        o_ref[q] = _dot(s, w_ref[:, cols].astype(BF16)) + b_ref[:, cols]


def _ada(c_sample, c_prompt, w_ada, b_ada):
    ns, nb = c_sample.shape[0], c_prompt.shape[0]
    return pl.pallas_call(
        _ada_kernel,
        grid=(N_MOD // ADA_GROUP,),
        in_specs=[
            pl.BlockSpec((ns, D_MODEL), lambda i: (0, 0)),
            pl.BlockSpec((nb, D_MODEL), lambda i: (0, 0)),
            pl.BlockSpec((D_MODEL, ADA_GROUP * D_MODEL), lambda i: (0, i)),
            pl.BlockSpec((1, ADA_GROUP * D_MODEL), lambda i: (0, i)),
        ],
        out_specs=pl.BlockSpec((ADA_GROUP, ns + nb, D_MODEL), lambda i: (i, 0, 0)),
        out_shape=jax.ShapeDtypeStruct((N_MOD, ns + nb, D_MODEL), F32),
        compiler_params=pltpu.CompilerParams(dimension_semantics=("arbitrary",), vmem_limit_bytes=VMEM_LIMIT),
        name="ada",
    )(c_sample, c_prompt, w_ada, b_ada.reshape(1, N_MOD * D_MODEL))


def _ffn_chunk_copies(c, slot, w_in_hbm, w_out_hbm, stage_in, stage_out, sems):
    c0 = c * FF_CHUNK
    return (
        pltpu.make_async_copy(w_in_hbm.at[:, pl.ds(c0, FF_CHUNK)], stage_in.at[slot, 0], sems.at[slot, 0]),
        pltpu.make_async_copy(w_in_hbm.at[:, pl.ds(D_FF + c0, FF_CHUNK)], stage_in.at[slot, 1], sems.at[slot, 1]),
        pltpu.make_async_copy(w_out_hbm.at[pl.ds(c0, FF_CHUNK), :], stage_out.at[slot], sems.at[slot, 2]),
    )


def _ffn_tile(x, sc, sh, gt, w_in_bf, w_out_bf, before_chunk, after_gate):
    h = (x * (1.0 + sc) + sh).astype(BF16)
    acc = jnp.zeros(x.shape, F32)
    for c in range(N_FF_CHUNKS):
        before_chunk(c)
        c0 = c * FF_CHUNK
        gate = _dot(h, w_in_bf[:, c0:c0 + FF_CHUNK])
        after_gate(c, gate)
        up = _dot(h, w_in_bf[:, D_FF + c0:D_FF + c0 + FF_CHUNK])
        act = (gate * jax.nn.sigmoid(gate) * up).astype(BF16)
        acc = acc + _dot(act, w_out_bf[c0:c0 + FF_CHUNK, :])
    return DN_ALPHA * x + FFN_RES * gt * acc


def _zero_from(v):
    bits = pltpu.bitcast(v, U32)
    return pltpu.bitcast(jnp.right_shift(jnp.right_shift(bits, 16), 16), F32)


def _ffn_kernel(tiles_per_seq, n_prompt_steps,
                x_ref, xs_ref, sc_ref, sh_ref, gt_ref, scs_ref, shs_ref, gts_ref, w_in_hbm, w_out_hbm, g_ref, b_ref,
                o_ref, os_ref, w_in_bf, w_out_bf, stage_in, stage_out, z_prev, sems):
    s = pl.program_id(0)
    tile = x_ref.shape[0]
    copies = functools.partial(_ffn_chunk_copies, w_in_hbm=w_in_hbm, w_out_hbm=w_out_hbm,
                               stage_in=stage_in, stage_out=stage_out, sems=sems)

    def load_chunk(c):
        slot = c % FFN_STAGE_SLOTS
        c0 = c * FF_CHUNK
        for copy in copies(c, slot):
            copy.wait()
        w_in_bf[:, c0:c0 + FF_CHUNK] = stage_in[slot, 0].astype(BF16)
        w_in_bf[:, D_FF + c0:D_FF + c0 + FF_CHUNK] = stage_in[slot, 1].astype(BF16)
        w_out_bf[c0:c0 + FF_CHUNK, :] = stage_out[slot].astype(BF16)
        if c + FFN_STAGE_SLOTS < N_FF_CHUNKS:
            for copy in copies(c + FFN_STAGE_SLOTS, slot):
                copy.start()

    def norm_prev_piece(c, gate):
        r0 = (c - 1) * FFN_LN_ROWS
        if c < 1 or r0 >= tile:
            return
        z = _tied_rows(z_prev[r0:r0 + FFN_LN_ROWS, :], gate)
        o_ref[r0:r0 + FFN_LN_ROWS, :] = _layer_norm(z, g_ref[...], b_ref[...])

    def nothing(*_):
        return None

    def prompt_tile(before_chunk, after_gate):
        seq = s // tiles_per_seq
        z = _ffn_tile(x_ref[...], sc_ref[pl.ds(seq, 1), :], sh_ref[pl.ds(seq, 1), :], gt_ref[pl.ds(seq, 1), :],
                      w_in_bf, w_out_bf, before_chunk, after_gate)
        z_prev[...] = z

    @pl.when(s == 0)
    def _():
        for c in range(FFN_STAGE_SLOTS):
            for copy in copies(c, c):
                copy.start()
        prompt_tile(load_chunk, nothing)

    @pl.when((s > 0) & (s < n_prompt_steps))
    def _():
        prompt_tile(nothing, norm_prev_piece)

    @pl.when(s == n_prompt_steps)
    def _():
        z = _ffn_tile(xs_ref[...], scs_ref[...], shs_ref[...], gts_ref[...], w_in_bf, w_out_bf,
                      nothing, norm_prev_piece)
        os_ref[...] = _layer_norm(z, g_ref[...], b_ref[...])


def _ffn(x, xs, mod, mod_idx, w_in, w_out, ln_g, ln_b, *, tile):
    nb, nt, _ = x.shape
    ns = xs.shape[0]
    assert tile % FFN_LN_ROWS == 0 and tile // FFN_LN_ROWS < N_FF_CHUNKS
    tiles_per_seq = nt // tile
    n_prompt_steps = nb * tiles_per_seq
    prompt_block = ns // nb

    def tile_index(s):
        p = jnp.minimum(s, n_prompt_steps - 1)
        return (p // tiles_per_seq, p % tiles_per_seq, 0)

    def prev_tile_index(s):
        return tile_index(jnp.maximum(s - 1, 0))

    tile_spec = pl.BlockSpec((None, tile, D_MODEL), tile_index)
    prev_tile_spec = pl.BlockSpec((None, tile, D_MODEL), prev_tile_index)
    rows_spec = pl.BlockSpec((ns, D_MODEL), lambda s: (0, 0))
    prompt_mod = [pl.BlockSpec((None, nb, D_MODEL), functools.partial(lambda s, i: (i, prompt_block, 0), i=i))
                  for i in mod_idx]
    sample_mod = [pl.BlockSpec((None, ns, D_MODEL), functools.partial(lambda s, i: (i, 0, 0), i=i))
                  for i in mod_idx]
    return pl.pallas_call(
        functools.partial(_ffn_kernel, tiles_per_seq, n_prompt_steps),
        grid=(n_prompt_steps + 1,),
        in_specs=[tile_spec, rows_spec] + prompt_mod + sample_mod + [
            pl.BlockSpec(memory_space=pl.ANY),
            pl.BlockSpec(memory_space=pl.ANY),
            _const_spec((1, D_MODEL)),
            _const_spec((1, D_MODEL)),
        ],
        out_specs=[prev_tile_spec, rows_spec],
        out_shape=[jax.ShapeDtypeStruct(x.shape, F32), jax.ShapeDtypeStruct(xs.shape, F32)],
        scratch_shapes=[
            pltpu.VMEM((D_MODEL, 2 * D_FF), BF16),
            pltpu.VMEM((D_FF, D_MODEL), BF16),
            pltpu.VMEM((FFN_STAGE_SLOTS, 2, D_MODEL, FF_CHUNK), F32),
            pltpu.VMEM((FFN_STAGE_SLOTS, FF_CHUNK, D_MODEL), F32),
            pltpu.VMEM((tile, D_MODEL), F32),
            pltpu.SemaphoreType.DMA((FFN_STAGE_SLOTS, 3)),
        ],
        compiler_params=pltpu.CompilerParams(dimension_semantics=("arbitrary",), vmem_limit_bytes=VMEM_LIMIT),
        name="ffn",
    )(x, xs, mod, mod, mod, mod, mod, mod, w_in, w_out, ln_g, ln_b)


def _tied_rows(z, after):
    tie = jnp.concatenate([_zero_from(after[0:SUBLANES, 0:LANES])] * (z.shape[0] // SUBLANES), axis=0)
    return jnp.concatenate([z[:, 0:LANES] + tie, z[:, LANES:]], axis=-1)


def _mixer_tail(x, gate_rows, conv, pooled, ga, gb, cg_ref, cb_ref, w_co_ref, pw_ref, ps_ref,
                w_po_ref, w_o_ref, interleave):
    n_groups = len(POOL_WINDOWS)
    pooled = pooled.astype(BF16)
    mixed = [_dot(pooled[:, g * POOL_GW:(g + 1) * POOL_GW], pw_ref[g]) for g in range(n_groups)]
    scaled = (jnp.concatenate(mixed, axis=-1) * ps_ref[...]).astype(BF16)
    yb = [_dot(scaled, w_po_ref[:, g * POOL_GW:(g + 1) * POOL_GW]) for g in range(n_groups)]
    if interleave:
        anchors = mixed + yb
        rows = conv.shape[0] // len(anchors)
        ya_in = jnp.concatenate(
            [_layer_norm(_tied_rows(conv[p * rows:(p + 1) * rows], anchors[p]), cg_ref[...], cb_ref[...])
             for p in range(len(anchors))], axis=0)
    else:
        ya_in = _layer_norm(conv, cg_ref[...], cb_ref[...])
    yb = jnp.concatenate(yb, axis=-1)
    ya_in = (ya_in * jax.nn.sigmoid(ya_in)).astype(BF16)
    merged = jax.nn.sigmoid(ga) * _dot(ya_in, w_co_ref[...])
    merged = (merged + jax.nn.sigmoid(gb) * yb).astype(BF16)
    return DN_ALPHA * x + gate_rows * _dot(merged, w_o_ref[...])


def _window_sum(hist, work, j, w, tile):
    end = HIST_PAD + tile
    lo = {w: HIST_PAD}
    v = w
    while v > 2:
        lo[v // 2] = (lo[v] - v // 2) // SUBLANES * SUBLANES
        v //= 2
    s = hist[j, lo[2]:end, :] + hist[j, lo[2] - 1:end - 1, :]
    v, slot = 2, 0
    while v < w:
        work[j, slot, lo[v]:end, :] = s
        s = work[j, slot, lo[2 * v]:end, :] + work[j, slot, lo[2 * v] - v:end - v, :]
        v, slot = 2 * v, 1 - slot
    return s


def _pack_rows(rows_f32):
    return pltpu.bitcast(rows_f32.astype(BF16), U32)


def _conv_chunk(conv_even, conv_odd, conv_wb, j, q0):
    n = CONV_ROWS // 2
    taps, weights = [], []
    for k in range(CONV_WIDTH):
        back = CONV_HIST - k
        if back % 2 == 0:
            words = conv_even[j, pl.ds(q0 - back // 2, n), :]
        else:
            words = conv_odd[j, pl.ds(q0 - (back - 1) // 2, n), :]
        taps.append(pltpu.bitcast(words, BF16))
        weights.append(jnp.concatenate([conv_wb[j, k]] * (CONV_ROWS // BF16_ROWS), axis=0))
    taps = jnp.stack(taps, axis=0).astype(F32)
    weights = jnp.stack(weights, axis=0).astype(F32)
    return jnp.sum(taps * weights, axis=0)


def _mixer_kernel(tiles_per_seq, n_tiles,
                  x_ref, sc_ref, sh_ref, gt_ref, w_in_ref, cw_ref, cbias_ref, cg_ref, cb_ref,
                  w_co_ref, pw_ref, ps_ref, w_po_ref, w_o_ref, g_ref, b_ref,
                  o_ref, nconv_ref, npool_ref,
                  conv_hist, conv_even, conv_odd, conv_wb, conv_out, pool_hist, work, z_prev):
    s = pl.program_id(0)

    @pl.when(s == n_tiles)
    def _():
        o_ref[...] = _layer_norm(z_prev[...], g_ref[...], b_ref[...])

    @pl.when(s < n_tiles)
    def _():
        _mixer_tile(s // tiles_per_seq, s % tiles_per_seq, tiles_per_seq, s == 0,
                    x_ref, sc_ref, sh_ref, gt_ref, w_in_ref, cw_ref, cbias_ref, cg_ref, cb_ref,
                    w_co_ref, pw_ref, ps_ref, w_po_ref, w_o_ref, g_ref, b_ref, o_ref, nconv_ref, npool_ref,
                    conv_hist, conv_even, conv_odd, conv_wb, conv_out, pool_hist, work, z_prev)


def _mixer_tile(b, t, tiles_per_seq, first_step,
                x_ref, sc_ref, sh_ref, gt_ref, w_in_ref, cw_ref, cbias_ref, cg_ref, cb_ref,
                w_co_ref, pw_ref, ps_ref, w_po_ref, w_o_ref, g_ref, b_ref, o_ref, nconv_ref, npool_ref,
                conv_hist, conv_even, conv_odd, conv_wb, conv_out, pool_hist, work, z_prev):
    tile = x_ref.shape[0]
    hist_words = HIST_PAD // 2
    tile_words = tile // 2

    @pl.when(first_step)
    def _():
        z_prev[...] = jnp.zeros(z_prev.shape, F32)
        for j in range(N_LANE_TILES):
            for k in range(CONV_WIDTH):
                conv_wb[j, k] = jnp.broadcast_to(cw_ref[k:k + 1, _lane_tile(j)], (BF16_ROWS, LANES)).astype(BF16)

    @pl.when(t == 0)
    def _():
        conv_hist[:, 0:HIST_PAD, :] = jnp.zeros((N_LANE_TILES, HIST_PAD, LANES), F32)
        pool_hist[:, 0:HIST_PAD, :] = jnp.zeros((N_LANE_TILES, HIST_PAD, LANES), F32)
        conv_even[:, 0:hist_words, :] = jnp.zeros((N_LANE_TILES, hist_words, LANES), U32)
        conv_odd[:, 0:hist_words, :] = jnp.zeros((N_LANE_TILES, hist_words, LANES), U32)

    @pl.when(t > 0)
    def _():
        conv_hist[:, 0:HIST_PAD, :] = conv_hist[:, tile:tile + HIST_PAD, :]
        pool_hist[:, 0:HIST_PAD, :] = pool_hist[:, tile:tile + HIST_PAD, :]
        conv_even[:, 0:hist_words, :] = conv_even[:, tile_words:tile_words + hist_words, :]
        conv_odd[:, 0:hist_words, :] = conv_odd[:, tile_words:tile_words + hist_words, :]

    def norm_prev_piece(p, after):
        r0 = p * MIX_LN_ROWS
        z = _tied_rows(z_prev[r0:r0 + MIX_LN_ROWS, :], after)
        o_ref[r0:r0 + MIX_LN_ROWS, :] = _layer_norm(z, g_ref[...], b_ref[...])

    x = x_ref[...]
    h = (x * (1.0 + sc_ref[pl.ds(b, 1), :]) + sh_ref[pl.ds(b, 1), :]).astype(BF16)

    parts = ([], [])
    for c in range(D_MODEL // POOL_GW):
        for half in range(2):
            c0 = half * D_MODEL + c * POOL_GW
            d = _dot(h, w_in_ref[:, c0:c0 + POOL_GW])
            parts[half].append(d)
            norm_prev_piece(2 * c + half, d)
    glu = jnp.concatenate(parts[0], axis=-1) * jax.nn.sigmoid(jnp.concatenate(parts[1], axis=-1))

    for j in range(N_LANE_TILES):
        rows = glu[:, _lane_tile(j)]
        conv_hist[j, HIST_PAD:HIST_PAD + tile, :] = rows
        conv_even[j, hist_words:hist_words + tile_words, :] = _pack_rows(rows)
    for j in range(N_LANE_TILES):
        conv_odd[j, hist_words:hist_words + tile_words, :] = _pack_rows(
            conv_hist[j, HIST_PAD - 1:HIST_PAD - 1 + tile, :])

    u = _dot(h, w_in_ref[:, 2 * D_MODEL:3 * D_MODEL])
    for j in range(N_LANE_TILES):
        pool_hist[j, HIST_PAD:HIST_PAD + tile, :] = u[:, _lane_tile(j)]
    pos1 = t * tile + 1 + lax.broadcasted_iota(jnp.int32, (POOL_MAX, LANES), 0)
    pooled = []
    for j in range(N_LANE_TILES):
        w = POOL_WINDOWS[j * LANES // POOL_GW]
        s = _window_sum(pool_hist, work, j, w, tile)
        head = s[0:POOL_MAX] / jnp.minimum(w, pos1).astype(F32)
        mean = jnp.concatenate([head, s[POOL_MAX:] * (1.0 / w)], axis=0)
        pooled.append(mean - u[:, _lane_tile(j)])

    gates = []
    for c in range(2 * D_MODEL // POOL_GW):
        zero = _zero_from(pooled[c][0:SUBLANES, :])
        zero = jnp.concatenate([zero, zero], axis=0).astype(BF16)
        lhs = jnp.concatenate(
            [jnp.concatenate([h[0:BF16_ROWS, 0:LANES] + zero, h[0:BF16_ROWS, LANES:]], axis=-1), h[BF16_ROWS:]],
            axis=0)
        c0 = 3 * D_MODEL + c * POOL_GW
        gates.append(_dot(lhs, w_in_ref[:, c0:c0 + POOL_GW]))
    n = D_MODEL // POOL_GW
    ga = jnp.concatenate(gates[0:n], axis=-1)
    gb = jnp.concatenate(gates[n:2 * n], axis=-1)
    pooled = jnp.concatenate(pooled, axis=-1)

    def conv_rows(c, carry):
        r0 = pl.multiple_of(c * CONV_ROWS, CONV_ROWS)
        q0 = hist_words + pl.multiple_of(c * (CONV_ROWS // 2), CONV_ROWS // 2)
        for j in range(N_LANE_TILES):
            conv_out[j, pl.ds(r0, CONV_ROWS), :] = _conv_chunk(conv_even, conv_odd, conv_wb, j, q0)
        return carry

    lax.fori_loop(0, tile // CONV_ROWS, conv_rows, 0)
    conv = jnp.concatenate([conv_out[j] + cbias_ref[:, _lane_tile(j)] for j in range(N_LANE_TILES)], axis=-1)

    z_prev[...] = _mixer_tail(x, gt_ref[pl.ds(b, 1), :], conv, pooled, ga, gb, cg_ref, cb_ref,
                              w_co_ref, pw_ref, ps_ref, w_po_ref, w_o_ref, interleave=True)

    @pl.when(first_step)
    def _():
        nconv_ref[...] = jnp.zeros(nconv_ref.shape, F32)
        npool_ref[...] = jnp.zeros(npool_ref.shape, F32)

    @pl.when(t == tiles_per_seq - 1)
    def _():
        nb = nconv_ref.shape[1]
        mine = lax.broadcasted_iota(jnp.int32, (nb, LANES), 0) == b
        last = HIST_PAD + tile
        for hist, out_ref, n_rows in ((conv_hist, nconv_ref, CONV_HIST), (pool_hist, npool_ref, POOL_HIST)):
            for j in range(N_LANE_TILES):
                for r in range(n_rows):
                    src = last - n_rows + r
                    new = jnp.broadcast_to(hist[j, src:src + 1, :], (nb, LANES))
                    out_ref[r, :, _lane_tile(j)] = jnp.where(mine, new, out_ref[r, :, _lane_tile(j)])


def _mixer_weight_specs():
    return [
        _const_spec((CONV_WIDTH, D_MODEL)),
        _const_spec((1, D_MODEL)),
        _const_spec((1, D_MODEL)),
        _const_spec((1, D_MODEL)),
        _const_spec((D_MODEL, D_MODEL)),
        _const_spec((len(POOL_WINDOWS), POOL_GW, POOL_GW)),
        _const_spec((1, D_MODEL)),
        _const_spec((D_MODEL, D_MODEL)),
        _const_spec((D_MODEL, D_MODEL)),
        _const_spec((1, D_MODEL)),
        _const_spec((1, D_MODEL)),
    ]


def _mixer_prompt(x, mod, w_in, tail_w, *, tile):
    nb, nt, _ = x.shape
    assert tile >= HIST_PAD and tile % CONV_ROWS == 0 and nt % tile == 0
    assert tile % MIX_LN_ROWS == 0 and tile // MIX_LN_ROWS == 2 * (D_MODEL // POOL_GW)
    prompt_block = (mod.shape[1] - nb) // nb
    tiles_per_seq = nt // tile
    n_tiles = nb * tiles_per_seq

    def tile_index(s):
        p = jnp.clip(s, 0, n_tiles - 1)
        return (p // tiles_per_seq, p % tiles_per_seq, 0)

    mod_specs = [pl.BlockSpec((None, nb, D_MODEL), functools.partial(lambda s, i: (i, prompt_block, 0), i=i))
                 for i in (4, 3, 5)]
    hist = (N_LANE_TILES, HIST_PAD + tile, LANES)
    hist_packed = (N_LANE_TILES, (HIST_PAD + tile) // 2, LANES)
    return pl.pallas_call(
        functools.partial(_mixer_kernel, tiles_per_seq, n_tiles),
        grid=(n_tiles + 1,),
        in_specs=[pl.BlockSpec((None, tile, D_MODEL), tile_index)] + mod_specs
        + [_const_spec((D_MODEL, 5 * D_MODEL))] + _mixer_weight_specs(),
        out_specs=[
            pl.BlockSpec((None, tile, D_MODEL), lambda s: tile_index(s - 1)),
            pl.BlockSpec((CONV_HIST, nb, D_MODEL), lambda s: (0, 0, 0)),
            pl.BlockSpec((POOL_HIST, nb, D_MODEL), lambda s: (0, 0, 0)),
        ],
        out_shape=[
            jax.ShapeDtypeStruct(x.shape, F32),
            jax.ShapeDtypeStruct((CONV_HIST, nb, D_MODEL), F32),
            jax.ShapeDtypeStruct((POOL_HIST, nb, D_MODEL), F32),
        ],
        scratch_shapes=[
            pltpu.VMEM(hist, F32),
            pltpu.VMEM(hist_packed, U32),
            pltpu.VMEM(hist_packed, U32),
            pltpu.VMEM((N_LANE_TILES, CONV_WIDTH, BF16_ROWS, LANES), BF16),
            pltpu.VMEM((N_LANE_TILES, tile, LANES), F32),
            pltpu.VMEM(hist, F32),
            pltpu.VMEM((N_LANE_TILES, 2) + hist[1:], F32),
            pltpu.VMEM((tile, D_MODEL), F32),
        ],
        compiler_params=pltpu.CompilerParams(dimension_semantics=("arbitrary",), vmem_limit_bytes=VMEM_LIMIT),
        name="mixer",
    )(x, mod, mod, mod, w_in, *tail_w)


def _sample_proj_kernel(x_ref, sc_ref, sh_ref, w_in_ref, glu_ref, u_ref, ga_ref, gb_ref):
    h = (x_ref[...] * (1.0 + sc_ref[...]) + sh_ref[...]).astype(BF16)
    a = _dot(h, w_in_ref[:, 0:D_MODEL])
    glu_ref[...] = a * jax.nn.sigmoid(_dot(h, w_in_ref[:, D_MODEL:2 * D_MODEL]))
    u_ref[...] = _dot(h, w_in_ref[:, 2 * D_MODEL:3 * D_MODEL])
    ga_ref[...] = _dot(h, w_in_ref[:, 3 * D_MODEL:4 * D_MODEL])
    gb_ref[...] = _dot(h, w_in_ref[:, 4 * D_MODEL:5 * D_MODEL])


def _sample_state_kernel(sconv_ref, spool_ref, glu_ref, u_ref, cw_ref, cbias_ref,
                         nconv_ref, npool_ref, conv_ref, pooled_ref):
    glu = glu_ref[...]
    conv = cbias_ref[...] + glu * cw_ref[CONV_HIST:CONV_WIDTH, :]
    for k in range(CONV_HIST):
        conv = conv + sconv_ref[k] * cw_ref[k:k + 1, :]
    conv_ref[...] = conv
    nconv_ref[0:CONV_HIST - 1] = sconv_ref[1:CONV_HIST]
    nconv_ref[CONV_HIST - 1] = glu

    u = u_ref[...]
    w = jnp.left_shift(2, pl.program_id(0))
    s = u
    for back in range(1, POOL_MAX):
        s = s + jnp.where(back < w, spool_ref[POOL_HIST - back], 0.0)
    pooled_ref[...] = s / jnp.minimum(w, PAST_LEN + 1).astype(F32) - u
    npool_ref[0:POOL_HIST - 1] = spool_ref[1:POOL_HIST]
    npool_ref[POOL_HIST - 1] = u


def _sample_tail_kernel(x_ref, gt_ref, conv_ref, pooled_ref, ga_ref, gb_ref, cg_ref, cb_ref, w_co_ref,
                        pw_ref, ps_ref, w_po_ref, w_o_ref, g_ref, b_ref, o_ref):
    z = _mixer_tail(x_ref[...], gt_ref[...], conv_ref[...], pooled_ref[...], ga_ref[...], gb_ref[...],
                    cg_ref, cb_ref, w_co_ref, pw_ref, ps_ref, w_po_ref, w_o_ref, interleave=False)
    o_ref[...] = _layer_norm(z, g_ref[...], b_ref[...])


def _mixer_sample(x, mod, state_conv, state_pool, w_in, tail_w):
    ns = x.shape[0]
    row_spec = pl.BlockSpec((ns, D_MODEL), lambda i: (0, 0))
    row_shape = jax.ShapeDtypeStruct((ns, D_MODEL), F32)

    def mod_spec(i):
        return pl.BlockSpec((None, ns, D_MODEL), lambda _: (i, 0, 0))

    glu, u, ga, gb = pl.pallas_call(
        _sample_proj_kernel,
        grid=(1,),
        in_specs=[row_spec, mod_spec(4), mod_spec(3), _const_spec((D_MODEL, 5 * D_MODEL))],
        out_specs=[row_spec] * 4,
        out_shape=[row_shape] * 4,
        compiler_params=pltpu.CompilerParams(dimension_semantics=("arbitrary",), vmem_limit_bytes=VMEM_LIMIT),
        name="sample_proj",
    )(x, mod, mod, w_in)

    conv_w, conv_b = tail_w[0], tail_w[1]
    tok_spec = pl.BlockSpec((ns, POOL_GW), lambda g: (0, g))
    nconv, npool, conv, pooled = pl.pallas_call(
        _sample_state_kernel,
        grid=(len(POOL_WINDOWS),),
        in_specs=[
            pl.BlockSpec((CONV_HIST, ns, POOL_GW), lambda g: (0, 0, g)),
            pl.BlockSpec((POOL_HIST, ns, POOL_GW), lambda g: (0, 0, g)),
            tok_spec, tok_spec,
            pl.BlockSpec((CONV_WIDTH, POOL_GW), lambda g: (0, g)),
            pl.BlockSpec((1, POOL_GW), lambda g: (0, g)),
        ],
        out_specs=[
            pl.BlockSpec((CONV_HIST, ns, POOL_GW), lambda g: (0, 0, g)),
            pl.BlockSpec((POOL_HIST, ns, POOL_GW), lambda g: (0, 0, g)),
            tok_spec, tok_spec,
        ],
        out_shape=[
            jax.ShapeDtypeStruct((CONV_HIST, ns, D_MODEL), F32),
            jax.ShapeDtypeStruct((POOL_HIST, ns, D_MODEL), F32),
            row_shape, row_shape,
        ],
        compiler_params=pltpu.CompilerParams(dimension_semantics=("arbitrary",), vmem_limit_bytes=VMEM_LIMIT),
        name="sample_state",
    )(state_conv, state_pool, glu, u, conv_w, conv_b)

    out = pl.pallas_call(
        _sample_tail_kernel,
        grid=(1,),
        in_specs=[row_spec, mod_spec(5), row_spec, row_spec, row_spec, row_spec] + _mixer_weight_specs()[2:],
        out_specs=row_spec,
        out_shape=row_shape,
        compiler_params=pltpu.CompilerParams(dimension_semantics=("arbitrary",), vmem_limit_bytes=VMEM_LIMIT),
        name="sample_tail",
    )(x, mod, conv, pooled, ga, gb, *tail_w[2:])
    return out, nconv, npool


def _time_major(state):
    return jnp.transpose(state, (0, 2, 1, 3))[0]


def _batch_major(state):
    return jnp.transpose(state[None], (0, 2, 1, 3))


def kernel(x_prompt, x_sample, state_conv, state_pool, c_prompt, c_sample, w_ada, b_ada, ffn1_w_in, ffn1_w_out, ln1_g, ln1_b, w_in, conv_w, conv_b, conv_ln_g, conv_ln_b, w_conv_out, pool_w, pool_scale, w_pool_out, w_out, ln2_g, ln2_b, ffn2_w_in, ffn2_w_out, ln3_g, ln3_b):
    nb, nt, d = x_prompt.shape
    ns, st, _ = x_sample.shape
    assert d == D_MODEL and st == 1 and w_ada.shape[0] == DEPTH
    assert nt % TOKEN_TILE == 0 and nt % FFN_TILE == 0 and ns % nb == 0 and N_MOD % ADA_GROUP == 0

    mod = _ada(c_sample, c_prompt, w_ada.reshape(D_MODEL, N_MOD * D_MODEL), b_ada[0])

    row = lambda v: v[0].reshape(1, D_MODEL)
    ffn1 = (ffn1_w_in.reshape(D_MODEL, 2 * D_FF), ffn1_w_out.reshape(D_FF, D_MODEL), row(ln1_g), row(ln1_b))
    ffn2 = (ffn2_w_in.reshape(D_MODEL, 2 * D_FF), ffn2_w_out.reshape(D_FF, D_MODEL), row(ln3_g), row(ln3_b))
    w_in_b = w_in[0].astype(BF16)
    tail_w = (conv_w[0], row(conv_b), row(conv_ln_g), row(conv_ln_b), w_conv_out[0].astype(BF16),
              pool_w[0].astype(BF16), row(pool_scale), w_pool_out[0].astype(BF16), w_out[0].astype(BF16),
              row(ln2_g), row(ln2_b))

    xp, xs = _ffn(x_prompt, x_sample.reshape(ns, D_MODEL), mod, (1, 0, 2), *ffn1, tile=FFN_TILE)
    xp, nconv_p, npool_p = _mixer_prompt(xp, mod, w_in_b, tail_w, tile=TOKEN_TILE)
    xs, nconv_s, npool_s = _mixer_sample(xs, mod, _time_major(state_conv), _time_major(state_pool),
                                         w_in_b, tail_w)
    xp, xs = _ffn(xp, xs, mod, (7, 6, 8), *ffn2, tile=FFN_TILE)

    return (xp, xs.reshape(ns, 1, D_MODEL), _batch_major(nconv_p), _batch_major(npool_p),
            _batch_major(nconv_s), _batch_major(npool_s))
```

```python
import functools

import jax
import jax.numpy as jnp
from jax import lax
from jax.experimental import pallas as pl
from jax.experimental.pallas import tpu as pltpu

D_MODEL = 1024
CONV_WIDTH = 31
CONV_HIST = CONV_WIDTH - 1
POOL_WINDOWS = (2, 4, 8, 16)
POOL_GW = D_MODEL // len(POOL_WINDOWS)
POOL_MAX = 16
POOL_HIST = POOL_MAX - 1
D_FF = ((8 * D_MODEL // 3 + 127) // 128) * 128
N_MOD = 9
ADA_GROUP = 3
DEPTH = 1
DN_ALPHA = (2.0 * DEPTH) ** 0.25
FFN_RES = 0.5
LN_EPS = 1e-5
PAST_LEN = 16384

SUBLANES = 8
LANES = 128
BF16_ROWS = 2 * SUBLANES
N_LANE_TILES = D_MODEL // LANES
FF_CHUNK = 256
N_FF_CHUNKS = D_FF // FF_CHUNK
FFN_STAGE_SLOTS = 2
TOKEN_TILE = 512
FFN_TILE = 512
FFN_LN_ROWS = 64
MIX_LN_ROWS = 64
HIST_PAD = 32
CONV_ROWS = 256
VMEM_LIMIT = 56 * 1024 * 1024

BF16 = jnp.bfloat16
F32 = jnp.float32
U32 = jnp.uint32

assert all(w & (w - 1) == 0 for w in POOL_WINDOWS) and POOL_GW == 2 * LANES
assert HIST_PAD >= CONV_HIST and HIST_PAD % BF16_ROWS == 0


def _const_spec(shape):
    n = len(shape)
    return pl.BlockSpec(shape, lambda *_: (0,) * n, pipeline_mode=pl.Buffered(1))


def _dot(a, b):
    return jnp.dot(a, b, preferred_element_type=F32)


def _layer_norm(z, g, b):
    mu = jnp.mean(z, axis=-1, keepdims=True)
    zc = z - mu
    var = jnp.mean(zc * zc, axis=-1, keepdims=True)
    return zc * lax.rsqrt(var + LN_EPS) * g + b


def _lane_tile(j):
    return slice(j * LANES, (j + 1) * LANES)


def _ada_kernel(cs_ref, cp_ref, w_ref, b_ref, o_ref):
    c = jnp.concatenate([cs_ref[...], cp_ref[...]], axis=0)
    s = (c * jax.nn.sigmoid(c)).astype(BF16)
    for q in range(ADA_GROUP):
        cols = slice(q * D_MODEL, (q + 1) * D_MODEL)
        o_ref[q] = _dot(s, w_ref[:, cols].astype(BF16)) + b_ref[:, cols]


def _ada(c_sample, c_prompt, w_ada, b_ada):
    ns, nb = c_sample.shape[0], c_prompt.shape[0]
    return pl.pallas_call(
        _ada_kernel,
        grid=(N_MOD // ADA_GROUP,),
        in_specs=[
            pl.BlockSpec((ns, D_MODEL), lambda i: (0, 0)),
            pl.BlockSpec((nb, D_MODEL), lambda i: (0, 0)),
            pl.BlockSpec((D_MODEL, ADA_GROUP * D_MODEL), lambda i: (0, i)),
            pl.BlockSpec((1, ADA_GROUP * D_MODEL), lambda i: (0, i)),
        ],
        out_specs=pl.BlockSpec((ADA_GROUP, ns + nb, D_MODEL), lambda i: (i, 0, 0)),
        out_shape=jax.ShapeDtypeStruct((N_MOD, ns + nb, D_MODEL), F32),
        compiler_params=pltpu.CompilerParams(dimension_semantics=("arbitrary",), vmem_limit_bytes=VMEM_LIMIT),
        name="ada",
    )(c_sample, c_prompt, w_ada, b_ada.reshape(1, N_MOD * D_MODEL))


def _ffn_chunk_copies(c, slot, w_in_hbm, w_out_hbm, stage_in, stage_out, sems):
    c0 = c * FF_CHUNK
    return (
        pltpu.make_async_copy(w_in_hbm.at[:, pl.ds(c0, FF_CHUNK)], stage_in.at[slot, 0], sems.at[slot, 0]),
        pltpu.make_async_copy(w_in_hbm.at[:, pl.ds(D_FF + c0, FF_CHUNK)], stage_in.at[slot, 1], sems.at[slot, 1]),
        pltpu.make_async_copy(w_out_hbm.at[pl.ds(c0, FF_CHUNK), :], stage_out.at[slot], sems.at[slot, 2]),
    )


def _ffn_tile(x, sc, sh, gt, w_in_bf, w_out_bf, before_chunk, after_gate):
    h = (x * (1.0 + sc) + sh).astype(BF16)
    acc = jnp.zeros(x.shape, F32)
    for c in range(N_FF_CHUNKS):
        before_chunk(c)
        c0 = c * FF_CHUNK
        gate = _dot(h, w_in_bf[:, c0:c0 + FF_CHUNK])
        after_gate(c, gate)
        up = _dot(h, w_in_bf[:, D_FF + c0:D_FF + c0 + FF_CHUNK])
        act = (gate * jax.nn.sigmoid(gate) * up).astype(BF16)
        acc = acc + _dot(act, w_out_bf[c0:c0 + FF_CHUNK, :])
    return DN_ALPHA * x + FFN_RES * gt * acc


def _zero_from(v):
    bits = pltpu.bitcast(v, U32)
    return pltpu.bitcast(jnp.right_shift(jnp.right_shift(bits, 16), 16), F32)


def _ffn_kernel(tiles_per_seq, n_prompt_steps,
                x_ref, xs_ref, sc_ref, sh_ref, gt_ref, scs_ref, shs_ref, gts_ref, w_in_hbm, w_out_hbm, g_ref, b_ref,
                o_ref, os_ref, w_in_bf, w_out_bf, stage_in, stage_out, z_prev, sems):
    s = pl.program_id(0)
    tile = x_ref.shape[0]
    copies = functools.partial(_ffn_chunk_copies, w_in_hbm=w_in_hbm, w_out_hbm=w_out_hbm,
                               stage_in=stage_in, stage_out=stage_out, sems=sems)

    def load_chunk(c):
        slot = c % FFN_STAGE_SLOTS
        c0 = c * FF_CHUNK
        for copy in copies(c, slot):
            copy.wait()
        w_in_bf[:, c0:c0 + FF_CHUNK] = stage_in[slot, 0].astype(BF16)
        w_in_bf[:, D_FF + c0:D_FF + c0 + FF_CHUNK] = stage_in[slot, 1].astype(BF16)
        w_out_bf[c0:c0 + FF_CHUNK, :] = stage_out[slot].astype(BF16)
        if c + FFN_STAGE_SLOTS < N_FF_CHUNKS:
            for copy in copies(c + FFN_STAGE_SLOTS, slot):
                copy.start()

    def norm_prev_piece(c, gate):
        r0 = (c - 1) * FFN_LN_ROWS
        if c < 1 or r0 >= tile:
            return
        z = _tied_rows(z_prev[r0:r0 + FFN_LN_ROWS, :], gate)
        o_ref[r0:r0 + FFN_LN_ROWS, :] = _layer_norm(z, g_ref[...], b_ref[...])

    def nothing(*_):
        return None

    def prompt_tile(before_chunk, after_gate):
        seq = s // tiles_per_seq
        z = _ffn_tile(x_ref[...], sc_ref[pl.ds(seq, 1), :], sh_ref[pl.ds(seq, 1), :], gt_ref[pl.ds(seq, 1), :],
                      w_in_bf, w_out_bf, before_chunk, after_gate)
        z_prev[...] = z

    @pl.when(s == 0)
    def _():
        for c in range(FFN_STAGE_SLOTS):
            for copy in copies(c, c):
                copy.start()
        prompt_tile(load_chunk, nothing)

    @pl.when((s > 0) & (s < n_prompt_steps))
    def _():
        prompt_tile(nothing, norm_prev_piece)

    @pl.when(s == n_prompt_steps)
    def _():
        z = _ffn_tile(xs_ref[...], scs_ref[...], shs_ref[...], gts_ref[...], w_in_bf, w_out_bf,
                      nothing, norm_prev_piece)
        os_ref[...] = _layer_norm(z, g_ref[...], b_ref[...])


def _ffn(x, xs, mod, mod_idx, w_in, w_out, ln_g, ln_b, *, tile):
    nb, nt, _ = x.shape
    ns = xs.shape[0]
    assert tile % FFN_LN_ROWS == 0 and tile // FFN_LN_ROWS < N_FF_CHUNKS
    tiles_per_seq = nt // tile
    n_prompt_steps = nb * tiles_per_seq
    prompt_block = ns // nb

    def tile_index(s):
        p = jnp.minimum(s, n_prompt_steps - 1)
        return (p // tiles_per_seq, p % tiles_per_seq, 0)

    def prev_tile_index(s):
        return tile_index(jnp.maximum(s - 1, 0))

    tile_spec = pl.BlockSpec((None, tile, D_MODEL), tile_index)
    prev_tile_spec = pl.BlockSpec((None, tile, D_MODEL), prev_tile_index)
    rows_spec = pl.BlockSpec((ns, D_MODEL), lambda s: (0, 0))
    prompt_mod = [pl.BlockSpec((None, nb, D_MODEL), functools.partial(lambda s, i: (i, prompt_block, 0), i=i))
                  for i in mod_idx]
    sample_mod = [pl.BlockSpec((None, ns, D_MODEL), functools.partial(lambda s, i: (i, 0, 0), i=i))
                  for i in mod_idx]
    return pl.pallas_call(
        functools.partial(_ffn_kernel, tiles_per_seq, n_prompt_steps),
        grid=(n_prompt_steps + 1,),
        in_specs=[tile_spec, rows_spec] + prompt_mod + sample_mod + [
            pl.BlockSpec(memory_space=pl.ANY),
            pl.BlockSpec(memory_space=pl.ANY),
            _const_spec((1, D_MODEL)),
            _const_spec((1, D_MODEL)),
        ],
        out_specs=[prev_tile_spec, rows_spec],
        out_shape=[jax.ShapeDtypeStruct(x.shape, F32), jax.ShapeDtypeStruct(xs.shape, F32)],
        scratch_shapes=[
            pltpu.VMEM((D_MODEL, 2 * D_FF), BF16),
            pltpu.VMEM((D_FF, D_MODEL), BF16),
            pltpu.VMEM((FFN_STAGE_SLOTS, 2, D_MODEL, FF_CHUNK), F32),
            pltpu.VMEM((FFN_STAGE_SLOTS, FF_CHUNK, D_MODEL), F32),
            pltpu.VMEM((tile, D_MODEL), F32),
            pltpu.SemaphoreType.DMA((FFN_STAGE_SLOTS, 3)),
        ],
        compiler_params=pltpu.CompilerParams(dimension_semantics=("arbitrary",), vmem_limit_bytes=VMEM_LIMIT),
        name="ffn",
    )(x, xs, mod, mod, mod, mod, mod, mod, w_in, w_out, ln_g, ln_b)


def _tied_rows(z, after):
    tie = jnp.concatenate([_zero_from(after[0:SUBLANES, 0:LANES])] * (z.shape[0] // SUBLANES), axis=0)
    return jnp.concatenate([z[:, 0:LANES] + tie, z[:, LANES:]], axis=-1)


def _mixer_tail(x, gate_rows, conv, pooled, ga, gb, cg_ref, cb_ref, w_co_ref, pw_ref, ps_ref,
                w_po_ref, w_o_ref, interleave):
    n_groups = len(POOL_WINDOWS)
    pooled = pooled.astype(BF16)
    mixed = [_dot(pooled[:, g * POOL_GW:(g + 1) * POOL_GW], pw_ref[g]) for g in range(n_groups)]
    scaled = (jnp.concatenate(mixed, axis=-1) * ps_ref[...]).astype(BF16)
    yb = [_dot(scaled, w_po_ref[:, g * POOL_GW:(g + 1) * POOL_GW]) for g in range(n_groups)]
    if interleave:
        anchors = mixed + yb
        rows = conv.shape[0] // len(anchors)
        ya_in = jnp.concatenate(
            [_layer_norm(_tied_rows(conv[p * rows:(p + 1) * rows], anchors[p]), cg_ref[...], cb_ref[...])
             for p in range(len(anchors))], axis=0)
    else:
        ya_in = _layer_norm(conv, cg_ref[...], cb_ref[...])
    yb = jnp.concatenate(yb, axis=-1)
    ya_in = (ya_in * jax.nn.sigmoid(ya_in)).astype(BF16)
    merged = jax.nn.sigmoid(ga) * _dot(ya_in, w_co_ref[...])
    merged = (merged + jax.nn.sigmoid(gb) * yb).astype(BF16)
    return DN_ALPHA * x + gate_rows * _dot(merged, w_o_ref[...])


def _window_sum(hist, work, j, w, tile):
    end = HIST_PAD + tile
    lo = {w: HIST_PAD}
    v = w
    while v > 2:
        lo[v // 2] = (lo[v] - v // 2) // SUBLANES * SUBLANES
        v //= 2
    s = hist[j, lo[2]:end, :] + hist[j, lo[2] - 1:end - 1, :]
    v, slot = 2, 0
    while v < w:
        work[j, slot, lo[v]:end, :] = s
        s = work[j, slot, lo[2 * v]:end, :] + work[j, slot, lo[2 * v] - v:end - v, :]
        v, slot = 2 * v, 1 - slot
    return s


def _pack_rows(rows_f32):
    return pltpu.bitcast(rows_f32.astype(BF16), U32)


def _conv_chunk(conv_even, conv_odd, conv_wb, j, q0):
    loaded = {}

    def words_at(ref, which, offset):
        if (which, offset) not in loaded:
            loaded[which, offset] = ref[j, pl.ds(q0 + offset, SUBLANES), :]
        return loaded[which, offset]

    taps, weights = [], []
    for k in range(CONV_WIDTH):
        back = CONV_HIST - k
        ref, which, first = (conv_even, 0, -(back // 2)) if back % 2 == 0 else (conv_odd, 1, -((back - 1) // 2))
        words = jnp.concatenate(
            [words_at(ref, which, first + i) for i in range(0, CONV_ROWS // 2, SUBLANES)], axis=0)
        taps.append(pltpu.bitcast(words, BF16))
        weights.append(jnp.concatenate([conv_wb[j, k]] * (CONV_ROWS // BF16_ROWS), axis=0))
    taps = jnp.stack(taps, axis=0).astype(F32)
    weights = jnp.stack(weights, axis=0).astype(F32)
    return jnp.sum(taps * weights, axis=0)


def _mixer_kernel(tiles_per_seq, n_tiles,
                  x_ref, sc_ref, sh_ref, gt_ref, w_in_ref, cw_ref, cbias_ref, cg_ref, cb_ref,
                  w_co_ref, pw_ref, ps_ref, w_po_ref, w_o_ref, g_ref, b_ref,
                  o_ref, nconv_ref, npool_ref,
                  conv_hist, conv_even, conv_odd, conv_wb, conv_out, pool_hist, work, z_prev):
    s = pl.program_id(0)

    @pl.when(s == n_tiles)
    def _():
        o_ref[...] = _layer_norm(z_prev[...], g_ref[...], b_ref[...])

    @pl.when(s < n_tiles)
    def _():
        _mixer_tile(s // tiles_per_seq, s % tiles_per_seq, tiles_per_seq, s == 0,
                    x_ref, sc_ref, sh_ref, gt_ref, w_in_ref, cw_ref, cbias_ref, cg_ref, cb_ref,
                    w_co_ref, pw_ref, ps_ref, w_po_ref, w_o_ref, g_ref, b_ref, o_ref, nconv_ref, npool_ref,
                    conv_hist, conv_even, conv_odd, conv_wb, conv_out, pool_hist, work, z_prev)


def _mixer_tile(b, t, tiles_per_seq, first_step,
                x_ref, sc_ref, sh_ref, gt_ref, w_in_ref, cw_ref, cbias_ref, cg_ref, cb_ref,
                w_co_ref, pw_ref, ps_ref, w_po_ref, w_o_ref, g_ref, b_ref, o_ref, nconv_ref, npool_ref,
                conv_hist, conv_even, conv_odd, conv_wb, conv_out, pool_hist, work, z_prev):
    tile = x_ref.shape[0]
    hist_words = HIST_PAD // 2
    tile_words = tile // 2

    @pl.when(first_step)
    def _():
        z_prev[...] = jnp.zeros(z_prev.shape, F32)
        for j in range(N_LANE_TILES):
            for k in range(CONV_WIDTH):
                conv_wb[j, k] = jnp.broadcast_to(cw_ref[k:k + 1, _lane_tile(j)], (BF16_ROWS, LANES)).astype(BF16)

    @pl.when(t == 0)
    def _():
        conv_hist[:, 0:HIST_PAD, :] = jnp.zeros((N_LANE_TILES, HIST_PAD, LANES), F32)
        pool_hist[:, 0:HIST_PAD, :] = jnp.zeros((N_LANE_TILES, HIST_PAD, LANES), F32)
        conv_even[:, 0:hist_words, :] = jnp.zeros((N_LANE_TILES, hist_words, LANES), U32)
        conv_odd[:, 0:hist_words, :] = jnp.zeros((N_LANE_TILES, hist_words, LANES), U32)

    @pl.when(t > 0)
    def _():
        conv_hist[:, 0:HIST_PAD, :] = conv_hist[:, tile:tile + HIST_PAD, :]
        pool_hist[:, 0:HIST_PAD, :] = pool_hist[:, tile:tile + HIST_PAD, :]
        conv_even[:, 0:hist_words, :] = conv_even[:, tile_words:tile_words + hist_words, :]
        conv_odd[:, 0:hist_words, :] = conv_odd[:, tile_words:tile_words + hist_words, :]

    def norm_prev_piece(p, after):
        r0 = p * MIX_LN_ROWS
        z = _tied_rows(z_prev[r0:r0 + MIX_LN_ROWS, :], after)
        o_ref[r0:r0 + MIX_LN_ROWS, :] = _layer_norm(z, g_ref[...], b_ref[...])

    x = x_ref[...]
    h = (x * (1.0 + sc_ref[pl.ds(b, 1), :]) + sh_ref[pl.ds(b, 1), :]).astype(BF16)

    parts = ([], [])
    for c in range(D_MODEL // POOL_GW):
        for half in range(2):
            c0 = half * D_MODEL + c * POOL_GW
            d = _dot(h, w_in_ref[:, c0:c0 + POOL_GW])
            parts[half].append(d)
            norm_prev_piece(2 * c + half, d)
    glu = jnp.concatenate(parts[0], axis=-1) * jax.nn.sigmoid(jnp.concatenate(parts[1], axis=-1))

    for j in range(N_LANE_TILES):
        rows = glu[:, _lane_tile(j)]
        conv_hist[j, HIST_PAD:HIST_PAD + tile, :] = rows
        conv_even[j, hist_words:hist_words + tile_words, :] = _pack_rows(rows)
    for j in range(N_LANE_TILES):
        conv_odd[j, hist_words:hist_words + tile_words, :] = _pack_rows(
            conv_hist[j, HIST_PAD - 1:HIST_PAD - 1 + tile, :])

    u = _dot(h, w_in_ref[:, 2 * D_MODEL:3 * D_MODEL])
    for j in range(N_LANE_TILES):
        pool_hist[j, HIST_PAD:HIST_PAD + tile, :] = u[:, _lane_tile(j)]
    pos1 = t * tile + 1 + lax.broadcasted_iota(jnp.int32, (POOL_MAX, LANES), 0)
    pooled = []
    for j in range(N_LANE_TILES):
        w = POOL_WINDOWS[j * LANES // POOL_GW]
        s = _window_sum(pool_hist, work, j, w, tile)
        head = s[0:POOL_MAX] / jnp.minimum(w, pos1).astype(F32)
        mean = jnp.concatenate([head, s[POOL_MAX:] * (1.0 / w)], axis=0)
        pooled.append(mean - u[:, _lane_tile(j)])

    gates = []
    for c in range(2 * D_MODEL // POOL_GW):
        zero = _zero_from(pooled[c][0:SUBLANES, :])
        zero = jnp.concatenate([zero, zero], axis=0).astype(BF16)
        lhs = jnp.concatenate(
            [jnp.concatenate([h[0:BF16_ROWS, 0:LANES] + zero, h[0:BF16_ROWS, LANES:]], axis=-1), h[BF16_ROWS:]],
            axis=0)
        c0 = 3 * D_MODEL + c * POOL_GW
        gates.append(_dot(lhs, w_in_ref[:, c0:c0 + POOL_GW]))
    n = D_MODEL // POOL_GW
    ga = jnp.concatenate(gates[0:n], axis=-1)
    gb = jnp.concatenate(gates[n:2 * n], axis=-1)
    pooled = jnp.concatenate(pooled, axis=-1)

    def conv_rows(c, carry):
        r0 = pl.multiple_of(c * CONV_ROWS, CONV_ROWS)
        q0 = hist_words + pl.multiple_of(c * (CONV_ROWS // 2), CONV_ROWS // 2)
        for j in range(N_LANE_TILES):
            conv_out[j, pl.ds(r0, CONV_ROWS), :] = _conv_chunk(conv_even, conv_odd, conv_wb, j, q0)
        return carry

    lax.fori_loop(0, tile // CONV_ROWS, conv_rows, 0)
    conv = jnp.concatenate([conv_out[j] + cbias_ref[:, _lane_tile(j)] for j in range(N_LANE_TILES)], axis=-1)

    z_prev[...] = _mixer_tail(x, gt_ref[pl.ds(b, 1), :], conv, pooled, ga, gb, cg_ref, cb_ref,
                              w_co_ref, pw_ref, ps_ref, w_po_ref, w_o_ref, interleave=True)

    @pl.when(first_step)
    def _():
        nconv_ref[...] = jnp.zeros(nconv_ref.shape, F32)
        npool_ref[...] = jnp.zeros(npool_ref.shape, F32)

    @pl.when(t == tiles_per_seq - 1)
    def _():
        nb = nconv_ref.shape[1]
        mine = lax.broadcasted_iota(jnp.int32, (nb, LANES), 0) == b
        last = HIST_PAD + tile
        for hist, out_ref, n_rows in ((conv_hist, nconv_ref, CONV_HIST), (pool_hist, npool_ref, POOL_HIST)):
            for j in range(N_LANE_TILES):
                for r in range(n_rows):
                    src = last - n_rows + r
                    new = jnp.broadcast_to(hist[j, src:src + 1, :], (nb, LANES))
                    out_ref[r, :, _lane_tile(j)] = jnp.where(mine, new, out_ref[r, :, _lane_tile(j)])


def _mixer_weight_specs():
    return [
        _const_spec((CONV_WIDTH, D_MODEL)),
        _const_spec((1, D_MODEL)),
        _const_spec((1, D_MODEL)),
        _const_spec((1, D_MODEL)),
        _const_spec((D_MODEL, D_MODEL)),
        _const_spec((len(POOL_WINDOWS), POOL_GW, POOL_GW)),
        _const_spec((1, D_MODEL)),
        _const_spec((D_MODEL, D_MODEL)),
        _const_spec((D_MODEL, D_MODEL)),
        _const_spec((1, D_MODEL)),
        _const_spec((1, D_MODEL)),
    ]


def _mixer_prompt(x, mod, w_in, tail_w, *, tile):
    nb, nt, _ = x.shape
    assert tile >= HIST_PAD and tile % CONV_ROWS == 0 and nt % tile == 0
    assert tile % MIX_LN_ROWS == 0 and tile // MIX_LN_ROWS == 2 * (D_MODEL // POOL_GW)
    prompt_block = (mod.shape[1] - nb) // nb
    tiles_per_seq = nt // tile
    n_tiles = nb * tiles_per_seq

    def tile_index(s):
        p = jnp.clip(s, 0, n_tiles - 1)
        return (p // tiles_per_seq, p % tiles_per_seq, 0)

    mod_specs = [pl.BlockSpec((None, nb, D_MODEL), functools.partial(lambda s, i: (i, prompt_block, 0), i=i))
                 for i in (4, 3, 5)]
    hist = (N_LANE_TILES, HIST_PAD + tile, LANES)
    hist_packed = (N_LANE_TILES, (HIST_PAD + tile) // 2, LANES)
    return pl.pallas_call(
        functools.partial(_mixer_kernel, tiles_per_seq, n_tiles),
        grid=(n_tiles + 1,),
        in_specs=[pl.BlockSpec((None, tile, D_MODEL), tile_index)] + mod_specs
        + [_const_spec((D_MODEL, 5 * D_MODEL))] + _mixer_weight_specs(),
        out_specs=[
            pl.BlockSpec((None, tile, D_MODEL), lambda s: tile_index(s - 1)),
            pl.BlockSpec((CONV_HIST, nb, D_MODEL), lambda s: (0, 0, 0)),
            pl.BlockSpec((POOL_HIST, nb, D_MODEL), lambda s: (0, 0, 0)),
        ],
        out_shape=[
            jax.ShapeDtypeStruct(x.shape, F32),
            jax.ShapeDtypeStruct((CONV_HIST, nb, D_MODEL), F32),
            jax.ShapeDtypeStruct((POOL_HIST, nb, D_MODEL), F32),
        ],
        scratch_shapes=[
            pltpu.VMEM(hist, F32),
            pltpu.VMEM(hist_packed, U32),
            pltpu.VMEM(hist_packed, U32),
            pltpu.VMEM((N_LANE_TILES, CONV_WIDTH, BF16_ROWS, LANES), BF16),
            pltpu.VMEM((N_LANE_TILES, tile, LANES), F32),
            pltpu.VMEM(hist, F32),
            pltpu.VMEM((N_LANE_TILES, 2) + hist[1:], F32),
            pltpu.VMEM((tile, D_MODEL), F32),
        ],
        compiler_params=pltpu.CompilerParams(dimension_semantics=("arbitrary",), vmem_limit_bytes=VMEM_LIMIT),
        name="mixer",
    )(x, mod, mod, mod, w_in, *tail_w)


def _sample_proj_kernel(x_ref, sc_ref, sh_ref, w_in_ref, glu_ref, u_ref, ga_ref, gb_ref):
    h = (x_ref[...] * (1.0 + sc_ref[...]) + sh_ref[...]).astype(BF16)
    a = _dot(h, w_in_ref[:, 0:D_MODEL])
    glu_ref[...] = a * jax.nn.sigmoid(_dot(h, w_in_ref[:, D_MODEL:2 * D_MODEL]))
    u_ref[...] = _dot(h, w_in_ref[:, 2 * D_MODEL:3 * D_MODEL])
    ga_ref[...] = _dot(h, w_in_ref[:, 3 * D_MODEL:4 * D_MODEL])
    gb_ref[...] = _dot(h, w_in_ref[:, 4 * D_MODEL:5 * D_MODEL])


def _sample_state_kernel(sconv_ref, spool_ref, glu_ref, u_ref, cw_ref, cbias_ref,
                         nconv_ref, npool_ref, conv_ref, pooled_ref):
    glu = glu_ref[...]
    conv = cbias_ref[...] + glu * cw_ref[CONV_HIST:CONV_WIDTH, :]
    for k in range(CONV_HIST):
        conv = conv + sconv_ref[k] * cw_ref[k:k + 1, :]
    conv_ref[...] = conv
    nconv_ref[0:CONV_HIST - 1] = sconv_ref[1:CONV_HIST]
    nconv_ref[CONV_HIST - 1] = glu

    u = u_ref[...]
    w = jnp.left_shift(2, pl.program_id(0))
    s = u
    for back in range(1, POOL_MAX):
        s = s + jnp.where(back < w, spool_ref[POOL_HIST - back], 0.0)
    pooled_ref[...] = s / jnp.minimum(w, PAST_LEN + 1).astype(F32) - u
    npool_ref[0:POOL_HIST - 1] = spool_ref[1:POOL_HIST]
    npool_ref[POOL_HIST - 1] = u


def _sample_tail_kernel(x_ref, gt_ref, conv_ref, pooled_ref, ga_ref, gb_ref, cg_ref, cb_ref, w_co_ref,
                        pw_ref, ps_ref, w_po_ref, w_o_ref, g_ref, b_ref, o_ref):
    z = _mixer_tail(x_ref[...], gt_ref[...], conv_ref[...], pooled_ref[...], ga_ref[...], gb_ref[...],
                    cg_ref, cb_ref, w_co_ref, pw_ref, ps_ref, w_po_ref, w_o_ref, interleave=False)
    o_ref[...] = _layer_norm(z, g_ref[...], b_ref[...])


def _mixer_sample(x, mod, state_conv, state_pool, w_in, tail_w):
    ns = x.shape[0]
    row_spec = pl.BlockSpec((ns, D_MODEL), lambda i: (0, 0))
    row_shape = jax.ShapeDtypeStruct((ns, D_MODEL), F32)

    def mod_spec(i):
        return pl.BlockSpec((None, ns, D_MODEL), lambda _: (i, 0, 0))

    glu, u, ga, gb = pl.pallas_call(
        _sample_proj_kernel,
        grid=(1,),
        in_specs=[row_spec, mod_spec(4), mod_spec(3), _const_spec((D_MODEL, 5 * D_MODEL))],
        out_specs=[row_spec] * 4,
        out_shape=[row_shape] * 4,
        compiler_params=pltpu.CompilerParams(dimension_semantics=("arbitrary",), vmem_limit_bytes=VMEM_LIMIT),
        name="sample_proj",
    )(x, mod, mod, w_in)

    conv_w, conv_b = tail_w[0], tail_w[1]
    tok_spec = pl.BlockSpec((ns, POOL_GW), lambda g: (0, g))
    nconv, npool, conv, pooled = pl.pallas_call(
        _sample_state_kernel,
        grid=(len(POOL_WINDOWS),),
        in_specs=[
            pl.BlockSpec((CONV_HIST, ns, POOL_GW), lambda g: (0, 0, g)),
            pl.BlockSpec((POOL_HIST, ns, POOL_GW), lambda g: (0, 0, g)),
            tok_spec, tok_spec,
            pl.BlockSpec((CONV_WIDTH, POOL_GW), lambda g: (0, g)),
            pl.BlockSpec((1, POOL_GW), lambda g: (0, g)),
        ],
        out_specs=[
            pl.BlockSpec((CONV_HIST, ns, POOL_GW), lambda g: (0, 0, g)),
            pl.BlockSpec((POOL_HIST, ns, POOL_GW), lambda g: (0, 0, g)),
            tok_spec, tok_spec,
        ],
        out_shape=[
            jax.ShapeDtypeStruct((CONV_HIST, ns, D_MODEL), F32),
            jax.ShapeDtypeStruct((POOL_HIST, ns, D_MODEL), F32),
            row_shape, row_shape,
        ],
        compiler_params=pltpu.CompilerParams(dimension_semantics=("arbitrary",), vmem_limit_bytes=VMEM_LIMIT),
        name="sample_state",
    )(state_conv, state_pool, glu, u, conv_w, conv_b)

    out = pl.pallas_call(
        _sample_tail_kernel,
        grid=(1,),
        in_specs=[row_spec, mod_spec(5), row_spec, row_spec, row_spec, row_spec] + _mixer_weight_specs()[2:],
        out_specs=row_spec,
        out_shape=row_shape,
        compiler_params=pltpu.CompilerParams(dimension_semantics=("arbitrary",), vmem_limit_bytes=VMEM_LIMIT),
        name="sample_tail",
    )(x, mod, conv, pooled, ga, gb, *tail_w[2:])
    return out, nconv, npool


def _time_major(state):
    return jnp.transpose(state, (0, 2, 1, 3))[0]


def _batch_major(state):
    return jnp.transpose(state[None], (0, 2, 1, 3))


def kernel(x_prompt, x_sample, state_conv, state_pool, c_prompt, c_sample, w_ada, b_ada, ffn1_w_in, ffn1_w_out, ln1_g, ln1_b, w_in, conv_w, conv_b, conv_ln_g, conv_ln_b, w_conv_out, pool_w, pool_scale, w_pool_out, w_out, ln2_g, ln2_b, ffn2_w_in, ffn2_w_out, ln3_g, ln3_b):
    nb, nt, d = x_prompt.shape
    ns, st, _ = x_sample.shape
    assert d == D_MODEL and st == 1 and w_ada.shape[0] == DEPTH
    assert nt % TOKEN_TILE == 0 and nt % FFN_TILE == 0 and ns % nb == 0 and N_MOD % ADA_GROUP == 0

    mod = _ada(c_sample, c_prompt, w_ada.reshape(D_MODEL, N_MOD * D_MODEL), b_ada[0])

    row = lambda v: v[0].reshape(1, D_MODEL)
    ffn1 = (ffn1_w_in.reshape(D_MODEL, 2 * D_FF), ffn1_w_out.reshape(D_FF, D_MODEL), row(ln1_g), row(ln1_b))
    ffn2 = (ffn2_w_in.reshape(D_MODEL, 2 * D_FF), ffn2_w_out.reshape(D_FF, D_MODEL), row(ln3_g), row(ln3_b))
    w_in_b = w_in[0].astype(BF16)
    tail_w = (conv_w[0], row(conv_b), row(conv_ln_g), row(conv_ln_b), w_conv_out[0].astype(BF16),
              pool_w[0].astype(BF16), row(pool_scale), w_pool_out[0].astype(BF16), w_out[0].astype(BF16),
              row(ln2_g), row(ln2_b))

    xp, xs = _ffn(x_prompt, x_sample.reshape(ns, D_MODEL), mod, (1, 0, 2), *ffn1, tile=FFN_TILE)
    xp, nconv_p, npool_p = _mixer_prompt(xp, mod, w_in_b, tail_w, tile=TOKEN_TILE)
    xs, nconv_s, npool_s = _mixer_sample(xs, mod, _time_major(state_conv), _time_major(state_pool),
                                         w_in_b, tail_w)
    xp, xs = _ffn(xp, xs, mod, (7, 6, 8), *ffn2, tile=FFN_TILE)

    return (xp, xs.reshape(ns, 1, D_MODEL), _batch_major(nconv_p), _batch_major(npool_p),
            _batch_major(nconv_s), _batch_major(npool_s))
```

```python
import functools

import jax
import jax.numpy as jnp
from jax import lax
from jax.experimental import pallas as pl
from jax.experimental.pallas import tpu as pltpu

D_MODEL = 1024
CONV_WIDTH = 31
CONV_HIST = CONV_WIDTH - 1
POOL_WINDOWS = (2, 4, 8, 16)
POOL_GW = D_MODEL // len(POOL_WINDOWS)
POOL_MAX = 16
POOL_HIST = POOL_MAX - 1
D_FF = ((8 * D_MODEL // 3 + 127) // 128) * 128
N_MOD = 9
ADA_GROUP = 3
DEPTH = 1
DN_ALPHA = (2.0 * DEPTH) ** 0.25
FFN_RES = 0.5
LN_EPS = 1e-5
PAST_LEN = 16384

SUBLANES = 8
LANES = 128
BF16_ROWS = 2 * SUBLANES
N_LANE_TILES = D_MODEL // LANES
FF_CHUNK = 256
N_FF_CHUNKS = D_FF // FF_CHUNK
FFN_STAGE_SLOTS = 2
TOKEN_TILE = 512
FFN_TILE = 512
FFN_LN_ROWS = 64
MIX_LN_ROWS = 64
HIST_PAD = 64
CONV_ROWS = 256
VMEM_LIMIT = 56 * 1024 * 1024

BF16 = jnp.bfloat16
F32 = jnp.float32
U32 = jnp.uint32

assert all(w & (w - 1) == 0 for w in POOL_WINDOWS) and POOL_GW == 2 * LANES
assert HIST_PAD >= CONV_HIST and HIST_PAD % BF16_ROWS == 0


def _const_spec(shape):
    n = len(shape)
    return pl.BlockSpec(shape, lambda *_: (0,) * n, pipeline_mode=pl.Buffered(1))


def _dot(a, b):
    return jnp.dot(a, b, preferred_element_type=F32)


def _layer_norm(z, g, b):
    mu = jnp.mean(z, axis=-1, keepdims=True)
    zc = z - mu
    var = jnp.mean(zc * zc, axis=-1, keepdims=True)
    return zc * lax.rsqrt(var + LN_EPS) * g + b


def _lane_tile(j):
    return slice(j * LANES, (j + 1) * LANES)


def _ada_kernel(cs_ref, cp_ref, w_ref, b_ref, o_ref):
    c = jnp.concatenate([cs_ref[...], cp_ref[...]], axis=0)
    s = (c * jax.nn.sigmoid(c)).astype(BF16)
    for q in range(ADA_GROUP):
        cols = slice(q * D_MODEL, (q + 1) * D_MODEL)
        o_ref[q] = _dot(s, w_ref[:, cols].astype(BF16)) + b_ref[:, cols]


def _ada(c_sample, c_prompt, w_ada, b_ada):
    ns, nb = c_sample.shape[0], c_prompt.shape[0]
    return pl.pallas_call(
        _ada_kernel,
        grid=(N_MOD // ADA_GROUP,),
        in_specs=[
            pl.BlockSpec((ns, D_MODEL), lambda i: (0, 0)),
            pl.BlockSpec((nb, D_MODEL), lambda i: (0, 0)),
            pl.BlockSpec((D_MODEL, ADA_GROUP * D_MODEL), lambda i: (0, i)),
            pl.BlockSpec((1, ADA_GROUP * D_MODEL), lambda i: (0, i)),
        ],
        out_specs=pl.BlockSpec((ADA_GROUP, ns + nb, D_MODEL), lambda i: (i, 0, 0)),
        out_shape=jax.ShapeDtypeStruct((N_MOD, ns + nb, D_MODEL), F32),
        compiler_params=pltpu.CompilerParams(dimension_semantics=("arbitrary",), vmem_limit_bytes=VMEM_LIMIT),
        name="ada",
    )(c_sample, c_prompt, w_ada, b_ada.reshape(1, N_MOD * D_MODEL))


def _ffn_chunk_copies(c, slot, w_in_hbm, w_out_hbm, stage_in, stage_out, sems):
    c0 = c * FF_CHUNK
    return (
        pltpu.make_async_copy(w_in_hbm.at[:, pl.ds(c0, FF_CHUNK)], stage_in.at[slot, 0], sems.at[slot, 0]),
        pltpu.make_async_copy(w_in_hbm.at[:, pl.ds(D_FF + c0, FF_CHUNK)], stage_in.at[slot, 1], sems.at[slot, 1]),
        pltpu.make_async_copy(w_out_hbm.at[pl.ds(c0, FF_CHUNK), :], stage_out.at[slot], sems.at[slot, 2]),
    )


def _ffn_tile(x, sc, sh, gt, w_in_bf, w_out_bf, before_chunk, after_gate):
    h = (x * (1.0 + sc) + sh).astype(BF16)
    acc = jnp.zeros(x.shape, F32)
    for c in range(N_FF_CHUNKS):
        before_chunk(c)
        c0 = c * FF_CHUNK
        gate = _dot(h, w_in_bf[:, c0:c0 + FF_CHUNK])
        after_gate(c, gate)
        up = _dot(h, w_in_bf[:, D_FF + c0:D_FF + c0 + FF_CHUNK])
        act = (gate * jax.nn.sigmoid(gate) * up).astype(BF16)
        acc = acc + _dot(act, w_out_bf[c0:c0 + FF_CHUNK, :])
    return DN_ALPHA * x + FFN_RES * gt * acc


def _zero_from(v):
    bits = pltpu.bitcast(v, U32)
    return pltpu.bitcast(jnp.right_shift(jnp.right_shift(bits, 16), 16), F32)


def _ffn_kernel(tiles_per_seq, n_prompt_steps,
                x_ref, xs_ref, sc_ref, sh_ref, gt_ref, scs_ref, shs_ref, gts_ref, w_in_hbm, w_out_hbm, g_ref, b_ref,
                o_ref, os_ref, w_in_bf, w_out_bf, stage_in, stage_out, z_prev, sems):
    s = pl.program_id(0)
    tile = x_ref.shape[0]
    copies = functools.partial(_ffn_chunk_copies, w_in_hbm=w_in_hbm, w_out_hbm=w_out_hbm,
                               stage_in=stage_in, stage_out=stage_out, sems=sems)

    def load_chunk(c):
        slot = c % FFN_STAGE_SLOTS
        c0 = c * FF_CHUNK
        for copy in copies(c, slot):
            copy.wait()
        w_in_bf[:, c0:c0 + FF_CHUNK] = stage_in[slot, 0].astype(BF16)
        w_in_bf[:, D_FF + c0:D_FF + c0 + FF_CHUNK] = stage_in[slot, 1].astype(BF16)
        w_out_bf[c0:c0 + FF_CHUNK, :] = stage_out[slot].astype(BF16)
        if c + FFN_STAGE_SLOTS < N_FF_CHUNKS:
            for copy in copies(c + FFN_STAGE_SLOTS, slot):
                copy.start()

    def norm_prev_piece(c, gate):
        r0 = (c - 1) * FFN_LN_ROWS
        if c < 1 or r0 >= tile:
            return
        z = _tied_rows(z_prev[r0:r0 + FFN_LN_ROWS, :], gate)
        o_ref[r0:r0 + FFN_LN_ROWS, :] = _layer_norm(z, g_ref[...], b_ref[...])

    def nothing(*_):
        return None

    def prompt_tile(before_chunk, after_gate):
        seq = s // tiles_per_seq
        z = _ffn_tile(x_ref[...], sc_ref[pl.ds(seq, 1), :], sh_ref[pl.ds(seq, 1), :], gt_ref[pl.ds(seq, 1), :],
                      w_in_bf, w_out_bf, before_chunk, after_gate)
        z_prev[...] = z

    @pl.when(s == 0)
    def _():
        for c in range(FFN_STAGE_SLOTS):
            for copy in copies(c, c):
                copy.start()
        prompt_tile(load_chunk, nothing)

    @pl.when((s > 0) & (s < n_prompt_steps))
    def _():
        prompt_tile(nothing, norm_prev_piece)

    @pl.when(s == n_prompt_steps)
    def _():
        z = _ffn_tile(xs_ref[...], scs_ref[...], shs_ref[...], gts_ref[...], w_in_bf, w_out_bf,
                      nothing, norm_prev_piece)
        os_ref[...] = _layer_norm(z, g_ref[...], b_ref[...])


def _ffn(x, xs, mod, mod_idx, w_in, w_out, ln_g, ln_b, *, tile):
    nb, nt, _ = x.shape
    ns = xs.shape[0]
    assert tile % FFN_LN_ROWS == 0 and tile // FFN_LN_ROWS < N_FF_CHUNKS
    tiles_per_seq = nt // tile
    n_prompt_steps = nb * tiles_per_seq
    prompt_block = ns // nb

    def tile_index(s):
        p = jnp.minimum(s, n_prompt_steps - 1)
        return (p // tiles_per_seq, p % tiles_per_seq, 0)

    def prev_tile_index(s):
        return tile_index(jnp.maximum(s - 1, 0))

    tile_spec = pl.BlockSpec((None, tile, D_MODEL), tile_index)
    prev_tile_spec = pl.BlockSpec((None, tile, D_MODEL), prev_tile_index)
    rows_spec = pl.BlockSpec((ns, D_MODEL), lambda s: (0, 0))
    prompt_mod = [pl.BlockSpec((None, nb, D_MODEL), functools.partial(lambda s, i: (i, prompt_block, 0), i=i))
                  for i in mod_idx]
    sample_mod = [pl.BlockSpec((None, ns, D_MODEL), functools.partial(lambda s, i: (i, 0, 0), i=i))
                  for i in mod_idx]
    return pl.pallas_call(
        functools.partial(_ffn_kernel, tiles_per_seq, n_prompt_steps),
        grid=(n_prompt_steps + 1,),
        in_specs=[tile_spec, rows_spec] + prompt_mod + sample_mod + [
            pl.BlockSpec(memory_space=pl.ANY),
            pl.BlockSpec(memory_space=pl.ANY),
            _const_spec((1, D_MODEL)),
            _const_spec((1, D_MODEL)),
        ],
        out_specs=[prev_tile_spec, rows_spec],
        out_shape=[jax.ShapeDtypeStruct(x.shape, F32), jax.ShapeDtypeStruct(xs.shape, F32)],
        scratch_shapes=[
            pltpu.VMEM((D_MODEL, 2 * D_FF), BF16),
            pltpu.VMEM((D_FF, D_MODEL), BF16),
            pltpu.VMEM((FFN_STAGE_SLOTS, 2, D_MODEL, FF_CHUNK), F32),
            pltpu.VMEM((FFN_STAGE_SLOTS, FF_CHUNK, D_MODEL), F32),
            pltpu.VMEM((tile, D_MODEL), F32),
            pltpu.SemaphoreType.DMA((FFN_STAGE_SLOTS, 3)),
        ],
        compiler_params=pltpu.CompilerParams(dimension_semantics=("arbitrary",), vmem_limit_bytes=VMEM_LIMIT),
        name="ffn",
    )(x, xs, mod, mod, mod, mod, mod, mod, w_in, w_out, ln_g, ln_b)


def _tied_rows(z, after):
    tie = jnp.concatenate([_zero_from(after[0:SUBLANES, 0:LANES])] * (z.shape[0] // SUBLANES), axis=0)
    return jnp.concatenate([z[:, 0:LANES] + tie, z[:, LANES:]], axis=-1)


def _mixer_tail(x, gate_rows, conv, pooled, ga, gb, cg_ref, cb_ref, w_co_ref, pw_ref, ps_ref,
                w_po_ref, w_o_ref, interleave):
    n_groups = len(POOL_WINDOWS)
    pooled = pooled.astype(BF16)
    mixed = [_dot(pooled[:, g * POOL_GW:(g + 1) * POOL_GW], pw_ref[g]) for g in range(n_groups)]
    scaled = (jnp.concatenate(mixed, axis=-1) * ps_ref[...]).astype(BF16)
    yb = [_dot(scaled, w_po_ref[:, g * POOL_GW:(g + 1) * POOL_GW]) for g in range(n_groups)]
    if interleave:
        anchors = mixed + yb
        rows = conv.shape[0] // len(anchors)
        ya_in = jnp.concatenate(
            [_layer_norm(_tied_rows(conv[p * rows:(p + 1) * rows], anchors[p]), cg_ref[...], cb_ref[...])
             for p in range(len(anchors))], axis=0)
    else:
        ya_in = _layer_norm(conv, cg_ref[...], cb_ref[...])
    yb = jnp.concatenate(yb, axis=-1)
    ya_in = (ya_in * jax.nn.sigmoid(ya_in)).astype(BF16)
    merged = jax.nn.sigmoid(ga) * _dot(ya_in, w_co_ref[...])
    merged = (merged + jax.nn.sigmoid(gb) * yb).astype(BF16)
    return DN_ALPHA * x + gate_rows * _dot(merged, w_o_ref[...])


def _window_sum(hist, work, j, w, tile):
    end = HIST_PAD + tile
    lo = {w: HIST_PAD}
    v = w
    while v > 2:
        lo[v // 2] = (lo[v] - v // 2) // SUBLANES * SUBLANES
        v //= 2
    s = hist[j, lo[2]:end, :] + hist[j, lo[2] - 1:end - 1, :]
    v, slot = 2, 0
    while v < w:
        work[j, slot, lo[v]:end, :] = s
        s = work[j, slot, lo[2 * v]:end, :] + work[j, slot, lo[2 * v] - v:end - v, :]
        v, slot = 2 * v, 1 - slot
    return s


def _pack_rows(rows_f32):
    return pltpu.bitcast(rows_f32.astype(BF16), U32)


def _conv_chunk(conv_even, conv_odd, conv_wb, j, q0):
    loaded = {}

    def words_at(ref, which, offset):
        if (which, offset) not in loaded:
            loaded[which, offset] = ref[j, pl.ds(q0 + offset, SUBLANES), :]
        return loaded[which, offset]

    taps, weights = [], []
    for k in range(CONV_WIDTH):
        back = CONV_HIST - k
        ref, which, first = (conv_even, 0, -(back // 2)) if back % 2 == 0 else (conv_odd, 1, -((back - 1) // 2))
        words = jnp.concatenate(
            [words_at(ref, which, first + i) for i in range(0, CONV_ROWS // 2, SUBLANES)], axis=0)
        taps.append(pltpu.bitcast(words, BF16))
        weights.append(jnp.concatenate([conv_wb[j, k]] * (CONV_ROWS // BF16_ROWS), axis=0))
    taps = jnp.stack(taps, axis=0).astype(F32)
    weights = jnp.stack(weights, axis=0).astype(F32)
    return jnp.sum(taps * weights, axis=0)


def _mixer_kernel(tiles_per_seq, n_tiles,
                  x_ref, sc_ref, sh_ref, gt_ref, w_in_ref, cw_ref, cbias_ref, cg_ref, cb_ref,
                  w_co_ref, pw_ref, ps_ref, w_po_ref, w_o_ref, g_ref, b_ref,
                  o_ref, nconv_ref, npool_ref,
                  conv_hist, conv_even, conv_odd, conv_wb, conv_out, pool_hist, work, z_prev):
    s = pl.program_id(0)

    @pl.when(s == n_tiles)
    def _():
        o_ref[...] = _layer_norm(z_prev[...], g_ref[...], b_ref[...])

    @pl.when(s < n_tiles)
    def _():
        _mixer_tile(s // tiles_per_seq, s % tiles_per_seq, tiles_per_seq, s == 0,
                    x_ref, sc_ref, sh_ref, gt_ref, w_in_ref, cw_ref, cbias_ref, cg_ref, cb_ref,
                    w_co_ref, pw_ref, ps_ref, w_po_ref, w_o_ref, g_ref, b_ref, o_ref, nconv_ref, npool_ref,
                    conv_hist, conv_even, conv_odd, conv_wb, conv_out, pool_hist, work, z_prev)


def _mixer_tile(b, t, tiles_per_seq, first_step,
                x_ref, sc_ref, sh_ref, gt_ref, w_in_ref, cw_ref, cbias_ref, cg_ref, cb_ref,
                w_co_ref, pw_ref, ps_ref, w_po_ref, w_o_ref, g_ref, b_ref, o_ref, nconv_ref, npool_ref,
                conv_hist, conv_even, conv_odd, conv_wb, conv_out, pool_hist, work, z_prev):
    tile = x_ref.shape[0]
    hist_words = HIST_PAD // 2
    tile_words = tile // 2

    @pl.when(first_step)
    def _():
        z_prev[...] = jnp.zeros(z_prev.shape, F32)
        for j in range(N_LANE_TILES):
            for k in range(CONV_WIDTH):
                conv_wb[j, k] = jnp.broadcast_to(cw_ref[k:k + 1, _lane_tile(j)], (BF16_ROWS, LANES)).astype(BF16)

    @pl.when(t == 0)
    def _():
        conv_hist[:, 0:HIST_PAD, :] = jnp.zeros((N_LANE_TILES, HIST_PAD, LANES), F32)
        pool_hist[:, 0:HIST_PAD, :] = jnp.zeros((N_LANE_TILES, HIST_PAD, LANES), F32)
        conv_even[:, 0:hist_words, :] = jnp.zeros((N_LANE_TILES, hist_words, LANES), U32)
        conv_odd[:, 0:hist_words, :] = jnp.zeros((N_LANE_TILES, hist_words, LANES), U32)

    @pl.when(t > 0)
    def _():
        conv_hist[:, 0:HIST_PAD, :] = conv_hist[:, tile:tile + HIST_PAD, :]
        pool_hist[:, 0:HIST_PAD, :] = pool_hist[:, tile:tile + HIST_PAD, :]
        conv_even[:, 0:hist_words, :] = conv_even[:, tile_words:tile_words + hist_words, :]
        conv_odd[:, 0:hist_words, :] = conv_odd[:, tile_words:tile_words + hist_words, :]

    def norm_prev_piece(p, after):
        r0 = p * MIX_LN_ROWS
        z = _tied_rows(z_prev[r0:r0 + MIX_LN_ROWS, :], after)
        o_ref[r0:r0 + MIX_LN_ROWS, :] = _layer_norm(z, g_ref[...], b_ref[...])

    x = x_ref[...]
    h = (x * (1.0 + sc_ref[pl.ds(b, 1), :]) + sh_ref[pl.ds(b, 1), :]).astype(BF16)

    parts = ([], [])
    for c in range(D_MODEL // POOL_GW):
        for half in range(2):
            c0 = half * D_MODEL + c * POOL_GW
            d = _dot(h, w_in_ref[:, c0:c0 + POOL_GW])
            parts[half].append(d)
            norm_prev_piece(2 * c + half, d)
    glu = jnp.concatenate(parts[0], axis=-1) * jax.nn.sigmoid(jnp.concatenate(parts[1], axis=-1))

    for j in range(N_LANE_TILES):
        rows = glu[:, _lane_tile(j)]
        conv_hist[j, HIST_PAD:HIST_PAD + tile, :] = rows
        conv_even[j, hist_words:hist_words + tile_words, :] = _pack_rows(rows)
    for j in range(N_LANE_TILES):
        conv_odd[j, hist_words:hist_words + tile_words, :] = _pack_rows(
            conv_hist[j, HIST_PAD - 1:HIST_PAD - 1 + tile, :])

    u = _dot(h, w_in_ref[:, 2 * D_MODEL:3 * D_MODEL])
    for j in range(N_LANE_TILES):
        pool_hist[j, HIST_PAD:HIST_PAD + tile, :] = u[:, _lane_tile(j)]
    pos1 = t * tile + 1 + lax.broadcasted_iota(jnp.int32, (POOL_MAX, LANES), 0)
    pooled = []
    for j in range(N_LANE_TILES):
        w = POOL_WINDOWS[j * LANES // POOL_GW]
        s = _window_sum(pool_hist, work, j, w, tile)
        head = s[0:POOL_MAX] / jnp.minimum(w, pos1).astype(F32)
        mean = jnp.concatenate([head, s[POOL_MAX:] * (1.0 / w)], axis=0)
        pooled.append(mean - u[:, _lane_tile(j)])

    gates = []
    for c in range(2 * D_MODEL // POOL_GW):
        zero = _zero_from(pooled[c][0:SUBLANES, :])
        zero = jnp.concatenate([zero, zero], axis=0).astype(BF16)
        lhs = jnp.concatenate(
            [jnp.concatenate([h[0:BF16_ROWS, 0:LANES] + zero, h[0:BF16_ROWS, LANES:]], axis=-1), h[BF16_ROWS:]],
            axis=0)
        c0 = 3 * D_MODEL + c * POOL_GW
        gates.append(_dot(lhs, w_in_ref[:, c0:c0 + POOL_GW]))
    n = D_MODEL // POOL_GW
    ga = jnp.concatenate(gates[0:n], axis=-1)
    gb = jnp.concatenate(gates[n:2 * n], axis=-1)
    pooled = jnp.concatenate(pooled, axis=-1)

    def conv_rows(c, carry):
        r0 = pl.multiple_of(c * CONV_ROWS, CONV_ROWS)
        q0 = hist_words + pl.multiple_of(c * (CONV_ROWS // 2), CONV_ROWS // 2)
        for j in range(N_LANE_TILES):
            conv_out[j, pl.ds(r0, CONV_ROWS), :] = _conv_chunk(conv_even, conv_odd, conv_wb, j, q0)
        return carry

    lax.fori_loop(0, tile // CONV_ROWS, conv_rows, 0)
    conv = jnp.concatenate([conv_out[j] + cbias_ref[:, _lane_tile(j)] for j in range(N_LANE_TILES)], axis=-1)

    z_prev[...] = _mixer_tail(x, gt_ref[pl.ds(b, 1), :], conv, pooled, ga, gb, cg_ref, cb_ref,
                              w_co_ref, pw_ref, ps_ref, w_po_ref, w_o_ref, interleave=True)

    @pl.when(first_step)
    def _():
        nconv_ref[...] = jnp.zeros(nconv_ref.shape, F32)
        npool_ref[...] = jnp.zeros(npool_ref.shape, F32)

    @pl.when(t == tiles_per_seq - 1)
    def _():
        nb = nconv_ref.shape[1]
        mine = lax.broadcasted_iota(jnp.int32, (nb, LANES), 0) == b
        last = HIST_PAD + tile
        for hist, out_ref, n_rows in ((conv_hist, nconv_ref, CONV_HIST), (pool_hist, npool_ref, POOL_HIST)):
            for j in range(N_LANE_TILES):
                for r in range(n_rows):
                    src = last - n_rows + r
                    new = jnp.broadcast_to(hist[j, src:src + 1, :], (nb, LANES))
                    out_ref[r, :, _lane_tile(j)] = jnp.where(mine, new, out_ref[r, :, _lane_tile(j)])


def _mixer_weight_specs():
    return [
        _const_spec((CONV_WIDTH, D_MODEL)),
        _const_spec((1, D_MODEL)),
        _const_spec((1, D_MODEL)),
        _const_spec((1, D_MODEL)),
        _const_spec((D_MODEL, D_MODEL)),
        _const_spec((len(POOL_WINDOWS), POOL_GW, POOL_GW)),
        _const_spec((1, D_MODEL)),
        _const_spec((D_MODEL, D_MODEL)),
        _const_spec((D_MODEL, D_MODEL)),
        _const_spec((1, D_MODEL)),
        _const_spec((1, D_MODEL)),
    ]


def _mixer_prompt(x, mod, w_in, tail_w, *, tile):
    nb, nt, _ = x.shape
    assert tile >= HIST_PAD and tile % CONV_ROWS == 0 and nt % tile == 0
    assert tile % MIX_LN_ROWS == 0 and tile // MIX_LN_ROWS == 2 * (D_MODEL // POOL_GW)
    prompt_block = (mod.shape[1] - nb) // nb
    tiles_per_seq = nt // tile
    n_tiles = nb * tiles_per_seq

    def tile_index(s):
        p = jnp.clip(s, 0, n_tiles - 1)
        return (p // tiles_per_seq, p % tiles_per_seq, 0)

    mod_specs = [pl.BlockSpec((None, nb, D_MODEL), functools.partial(lambda s, i: (i, prompt_block, 0), i=i))
                 for i in (4, 3, 5)]
    hist = (N_LANE_TILES, HIST_PAD + tile, LANES)
    hist_packed = (N_LANE_TILES, (HIST_PAD + tile) // 2, LANES)
    return pl.pallas_call(
        functools.partial(_mixer_kernel, tiles_per_seq, n_tiles),
        grid=(n_tiles + 1,),
        in_specs=[pl.BlockSpec((None, tile, D_MODEL), tile_index)] + mod_specs
        + [_const_spec((D_MODEL, 5 * D_MODEL))] + _mixer_weight_specs(),
        out_specs=[
            pl.BlockSpec((None, tile, D_MODEL), lambda s: tile_index(s - 1)),
            pl.BlockSpec((CONV_HIST, nb, D_MODEL), lambda s: (0, 0, 0)),
            pl.BlockSpec((POOL_HIST, nb, D_MODEL), lambda s: (0, 0, 0)),
        ],
        out_shape=[
            jax.ShapeDtypeStruct(x.shape, F32),
            jax.ShapeDtypeStruct((CONV_HIST, nb, D_MODEL), F32),
            jax.ShapeDtypeStruct((POOL_HIST, nb, D_MODEL), F32),
        ],
        scratch_shapes=[
            pltpu.VMEM(hist, F32),
            pltpu.VMEM(hist_packed, U32),
            pltpu.VMEM(hist_packed, U32),
            pltpu.VMEM((N_LANE_TILES, CONV_WIDTH, BF16_ROWS, LANES), BF16),
            pltpu.VMEM((N_LANE_TILES, tile, LANES), F32),
            pltpu.VMEM(hist, F32),
            pltpu.VMEM((N_LANE_TILES, 2) + hist[1:], F32),
            pltpu.VMEM((tile, D_MODEL), F32),
        ],
        compiler_params=pltpu.CompilerParams(dimension_semantics=("arbitrary",), vmem_limit_bytes=VMEM_LIMIT),
        name="mixer",
    )(x, mod, mod, mod, w_in, *tail_w)


def _sample_proj_kernel(x_ref, sc_ref, sh_ref, w_in_ref, glu_ref, u_ref, ga_ref, gb_ref):
    h = (x_ref[...] * (1.0 + sc_ref[...]) + sh_ref[...]).astype(BF16)
    a = _dot(h, w_in_ref[:, 0:D_MODEL])
    glu_ref[...] = a * jax.nn.sigmoid(_dot(h, w_in_ref[:, D_MODEL:2 * D_MODEL]))
    u_ref[...] = _dot(h, w_in_ref[:, 2 * D_MODEL:3 * D_MODEL])
    ga_ref[...] = _dot(h, w_in_ref[:, 3 * D_MODEL:4 * D_MODEL])
    gb_ref[...] = _dot(h, w_in_ref[:, 4 * D_MODEL:5 * D_MODEL])


def _sample_state_kernel(sconv_ref, spool_ref, glu_ref, u_ref, cw_ref, cbias_ref,
                         nconv_ref, npool_ref, conv_ref, pooled_ref):
    glu = glu_ref[...]
    conv = cbias_ref[...] + glu * cw_ref[CONV_HIST:CONV_WIDTH, :]
    for k in range(CONV_HIST):
        conv = conv + sconv_ref[k] * cw_ref[k:k + 1, :]
    conv_ref[...] = conv
    nconv_ref[0:CONV_HIST - 1] = sconv_ref[1:CONV_HIST]
    nconv_ref[CONV_HIST - 1] = glu

    u = u_ref[...]
    w = jnp.left_shift(2, pl.program_id(0))
    s = u
    for back in range(1, POOL_MAX):
        s = s + jnp.where(back < w, spool_ref[POOL_HIST - back], 0.0)
    pooled_ref[...] = s / jnp.minimum(w, PAST_LEN + 1).astype(F32) - u
    npool_ref[0:POOL_HIST - 1] = spool_ref[1:POOL_HIST]
    npool_ref[POOL_HIST - 1] = u


def _sample_tail_kernel(x_ref, gt_ref, conv_ref, pooled_ref, ga_ref, gb_ref, cg_ref, cb_ref, w_co_ref,
                        pw_ref, ps_ref, w_po_ref, w_o_ref, g_ref, b_ref, o_ref):
    z = _mixer_tail(x_ref[...], gt_ref[...], conv_ref[...], pooled_ref[...], ga_ref[...], gb_ref[...],
                    cg_ref, cb_ref, w_co_ref, pw_ref, ps_ref, w_po_ref, w_o_ref, interleave=False)
    o_ref[...] = _layer_norm(z, g_ref[...], b_ref[...])


def _mixer_sample(x, mod, state_conv, state_pool, w_in, tail_w):
    ns = x.shape[0]
    row_spec = pl.BlockSpec((ns, D_MODEL), lambda i: (0, 0))
    row_shape = jax.ShapeDtypeStruct((ns, D_MODEL), F32)

    def mod_spec(i):
        return pl.BlockSpec((None, ns, D_MODEL), lambda _: (i, 0, 0))

    glu, u, ga, gb = pl.pallas_call(
        _sample_proj_kernel,
        grid=(1,),
        in_specs=[row_spec, mod_spec(4), mod_spec(3), _const_spec((D_MODEL, 5 * D_MODEL))],
        out_specs=[row_spec] * 4,
        out_shape=[row_shape] * 4,
        compiler_params=pltpu.CompilerParams(dimension_semantics=("arbitrary",), vmem_limit_bytes=VMEM_LIMIT),
        name="sample_proj",
    )(x, mod, mod, w_in)

    conv_w, conv_b = tail_w[0], tail_w[1]
    tok_spec = pl.BlockSpec((ns, POOL_GW), lambda g: (0, g))
    nconv, npool, conv, pooled = pl.pallas_call(
        _sample_state_kernel,
        grid=(len(POOL_WINDOWS),),
        in_specs=[
            pl.BlockSpec((CONV_HIST, ns, POOL_GW), lambda g: (0, 0, g)),
            pl.BlockSpec((POOL_HIST, ns, POOL_GW), lambda g: (0, 0, g)),
            tok_spec, tok_spec,
            pl.BlockSpec((CONV_WIDTH, POOL_GW), lambda g: (0, g)),
            pl.BlockSpec((1, POOL_GW), lambda g: (0, g)),
        ],
        out_specs=[
            pl.BlockSpec((CONV_HIST, ns, POOL_GW), lambda g: (0, 0, g)),
            pl.BlockSpec((POOL_HIST, ns, POOL_GW), lambda g: (0, 0, g)),
            tok_spec, tok_spec,
        ],
        out_shape=[
            jax.ShapeDtypeStruct((CONV_HIST, ns, D_MODEL), F32),
            jax.ShapeDtypeStruct((POOL_HIST, ns, D_MODEL), F32),
            row_shape, row_shape,
        ],
        compiler_params=pltpu.CompilerParams(dimension_semantics=("arbitrary",), vmem_limit_bytes=VMEM_LIMIT),
        name="sample_state",
    )(state_conv, state_pool, glu, u, conv_w, conv_b)

    out = pl.pallas_call(
        _sample_tail_kernel,
        grid=(1,),
        in_specs=[row_spec, mod_spec(5), row_spec, row_spec, row_spec, row_spec] + _mixer_weight_specs()[2:],
        out_specs=row_spec,
        out_shape=row_shape,
        compiler_params=pltpu.CompilerParams(dimension_semantics=("arbitrary",), vmem_limit_bytes=VMEM_LIMIT),
        name="sample_tail",
    )(x, mod, conv, pooled, ga, gb, *tail_w[2:])
    return out, nconv, npool


def _time_major(state):
    return jnp.transpose(state, (0, 2, 1, 3))[0]


def _batch_major(state):
    return jnp.transpose(state[None], (0, 2, 1, 3))


def kernel(x_prompt, x_sample, state_conv, state_pool, c_prompt, c_sample, w_ada, b_ada, ffn1_w_in, ffn1_w_out, ln1_g, ln1_b, w_in, conv_w, conv_b, conv_ln_g, conv_ln_b, w_conv_out, pool_w, pool_scale, w_pool_out, w_out, ln2_g, ln2_b, ffn2_w_in, ffn2_w_out, ln3_g, ln3_b):
    nb, nt, d = x_prompt.shape
    ns, st, _ = x_sample.shape
    assert d == D_MODEL and st == 1 and w_ada.shape[0] == DEPTH
    assert nt % TOKEN_TILE == 0 and nt % FFN_TILE == 0 and ns % nb == 0 and N_MOD % ADA_GROUP == 0

    mod = _ada(c_sample, c_prompt, w_ada.reshape(D_MODEL, N_MOD * D_MODEL), b_ada[0])

    row = lambda v: v[0].reshape(1, D_MODEL)
    ffn1 = (ffn1_w_in.reshape(D_MODEL, 2 * D_FF), ffn1_w_out.reshape(D_FF, D_MODEL), row(ln1_g), row(ln1_b))
    ffn2 = (ffn2_w_in.reshape(D_MODEL, 2 * D_FF), ffn2_w_out.reshape(D_FF, D_MODEL), row(ln3_g), row(ln3_b))
    w_in_b = w_in[0].astype(BF16)
    tail_w = (conv_w[0], row(conv_b), row(conv_ln_g), row(conv_ln_b), w_conv_out[0].astype(BF16),
              pool_w[0].astype(BF16), row(pool_scale), w_pool_out[0].astype(BF16), w_out[0].astype(BF16),
              row(ln2_g), row(ln2_b))

    xp, xs = _ffn(x_prompt, x_sample.reshape(ns, D_MODEL), mod, (1, 0, 2), *ffn1, tile=FFN_TILE)
    xp, nconv_p, npool_p = _mixer_prompt(xp, mod, w_in_b, tail_w, tile=TOKEN_TILE)
    xs, nconv_s, npool_s = _mixer_sample(xs, mod, _time_major(state_conv), _time_major(state_pool),
                                         w_in_b, tail_w)
    xp, xs = _ffn(xp, xs, mod, (7, 6, 8), *ffn2, tile=FFN_TILE)

    return (xp, xs.reshape(ns, 1, D_MODEL), _batch_major(nconv_p), _batch_major(npool_p),
            _batch_major(nconv_s), _batch_major(npool_s))
```

```python
import functools

import jax
import jax.numpy as jnp
from jax import lax
from jax.experimental import pallas as pl
from jax.experimental.pallas import tpu as pltpu

D_MODEL = 1024
CONV_WIDTH = 31
CONV_HIST = CONV_WIDTH - 1
POOL_WINDOWS = (2, 4, 8, 16)
POOL_GW = D_MODEL // len(POOL_WINDOWS)
POOL_MAX = 16
POOL_HIST = POOL_MAX - 1
D_FF = ((8 * D_MODEL // 3 + 127) // 128) * 128
N_MOD = 9
ADA_GROUP = 3
DEPTH = 1
DN_ALPHA = (2.0 * DEPTH) ** 0.25
FFN_RES = 0.5
LN_EPS = 1e-5
PAST_LEN = 16384

SUBLANES = 8
LANES = 128
BF16_ROWS = 2 * SUBLANES
N_LANE_TILES = D_MODEL // LANES
FF_CHUNK = 256
N_FF_CHUNKS = D_FF // FF_CHUNK
FFN_STAGE_SLOTS = 2
TOKEN_TILE = 512
FFN_TILE = 512
MIX_LN_ROWS = 64
HIST_PAD = 32
CONV_ROWS = 256
VMEM_LIMIT = 56 * 1024 * 1024

BF16 = jnp.bfloat16
F32 = jnp.float32
U32 = jnp.uint32

assert all(w & (w - 1) == 0 for w in POOL_WINDOWS) and POOL_GW == 2 * LANES
assert HIST_PAD >= CONV_HIST and HIST_PAD % BF16_ROWS == 0


def _const_spec(shape):
    n = len(shape)
    return pl.BlockSpec(shape, lambda *_: (0,) * n, pipeline_mode=pl.Buffered(1))


def _dot(a, b):
    return jnp.dot(a, b, preferred_element_type=F32)


def _layer_norm(z, g, b):
    mu = jnp.mean(z, axis=-1, keepdims=True)
    zc = z - mu
    var = jnp.mean(zc * zc, axis=-1, keepdims=True)
    return zc * lax.rsqrt(var + LN_EPS) * g + b


def _lane_tile(j):
    return slice(j * LANES, (j + 1) * LANES)


def _ada_kernel(cs_ref, cp_ref, w_ref, b_ref, o_ref):
    c = jnp.concatenate([cs_ref[...], cp_ref[...]], axis=0)
    s = (c * jax.nn.sigmoid(c)).astype(BF16)
    for q in range(ADA_GROUP):
        cols = slice(q * D_MODEL, (q + 1) * D_MODEL)
        o_ref[q] = _dot(s, w_ref[:, cols].astype(BF16)) + b_ref[:, cols]


def _ada(c_sample, c_prompt, w_ada, b_ada):
    ns, nb = c_sample.shape[0], c_prompt.shape[0]
    return pl.pallas_call(
        _ada_kernel,
        grid=(N_MOD // ADA_GROUP,),
        in_specs=[
            pl.BlockSpec((ns, D_MODEL), lambda i: (0, 0)),
            pl.BlockSpec((nb, D_MODEL), lambda i: (0, 0)),
            pl.BlockSpec((D_MODEL, ADA_GROUP * D_MODEL), lambda i: (0, i)),
            pl.BlockSpec((1, ADA_GROUP * D_MODEL), lambda i: (0, i)),
        ],
        out_specs=pl.BlockSpec((ADA_GROUP, ns + nb, D_MODEL), lambda i: (i, 0, 0)),
        out_shape=jax.ShapeDtypeStruct((N_MOD, ns + nb, D_MODEL), F32),
        compiler_params=pltpu.CompilerParams(dimension_semantics=("arbitrary",), vmem_limit_bytes=VMEM_LIMIT),
        name="ada",
    )(c_sample, c_prompt, w_ada, b_ada.reshape(1, N_MOD * D_MODEL))


def _ffn_chunk_copies(c, slot, w_in_hbm, w_out_hbm, stage_in, stage_out, sems):
    c0 = c * FF_CHUNK
    return (
        pltpu.make_async_copy(w_in_hbm.at[:, pl.ds(c0, FF_CHUNK)], stage_in.at[slot, 0], sems.at[slot, 0]),
        pltpu.make_async_copy(w_in_hbm.at[:, pl.ds(D_FF + c0, FF_CHUNK)], stage_in.at[slot, 1], sems.at[slot, 1]),
        pltpu.make_async_copy(w_out_hbm.at[pl.ds(c0, FF_CHUNK), :], stage_out.at[slot], sems.at[slot, 2]),
    )


def _ffn_tile(x, sc, sh, gt, w_in_bf, w_out_bf, g_ref, b_ref, prepare_chunk):
    h = (x * (1.0 + sc) + sh).astype(BF16)
    acc = jnp.zeros(x.shape, F32)
    for c in range(N_FF_CHUNKS):
        prepare_chunk(c)
        c0 = c * FF_CHUNK
        gate = _dot(h, w_in_bf[:, c0:c0 + FF_CHUNK])
        up = _dot(h, w_in_bf[:, D_FF + c0:D_FF + c0 + FF_CHUNK])
        act = (gate * jax.nn.sigmoid(gate) * up).astype(BF16)
        acc = acc + _dot(act, w_out_bf[c0:c0 + FF_CHUNK, :])
    z = DN_ALPHA * x + FFN_RES * gt * acc
    return _layer_norm(z, g_ref[...], b_ref[...])


def _ffn_kernel(tiles_per_seq, n_prompt_steps,
                x_ref, xs_ref, sc_ref, sh_ref, gt_ref, scs_ref, shs_ref, gts_ref, w_in_hbm, w_out_hbm, g_ref, b_ref,
                o_ref, os_ref, w_in_bf, w_out_bf, stage_in, stage_out, sems):
    s = pl.program_id(0)
    copies = functools.partial(_ffn_chunk_copies, w_in_hbm=w_in_hbm, w_out_hbm=w_out_hbm,
                               stage_in=stage_in, stage_out=stage_out, sems=sems)

    def load_chunk(c):
        slot = c % FFN_STAGE_SLOTS
        c0 = c * FF_CHUNK
        for copy in copies(c, slot):
            copy.wait()
        w_in_bf[:, c0:c0 + FF_CHUNK] = stage_in[slot, 0].astype(BF16)
        w_in_bf[:, D_FF + c0:D_FF + c0 + FF_CHUNK] = stage_in[slot, 1].astype(BF16)
        w_out_bf[c0:c0 + FF_CHUNK, :] = stage_out[slot].astype(BF16)
        if c + FFN_STAGE_SLOTS < N_FF_CHUNKS:
            for copy in copies(c + FFN_STAGE_SLOTS, slot):
                copy.start()

    def prompt_tile(prepare_chunk):
        seq = s // tiles_per_seq
        o_ref[...] = _ffn_tile(x_ref[...], sc_ref[pl.ds(seq, 1), :], sh_ref[pl.ds(seq, 1), :],
                               gt_ref[pl.ds(seq, 1), :], w_in_bf, w_out_bf, g_ref, b_ref, prepare_chunk)

    @pl.when(s == 0)
    def _():
        for c in range(FFN_STAGE_SLOTS):
            for copy in copies(c, c):
                copy.start()
        prompt_tile(load_chunk)

    @pl.when((s > 0) & (s < n_prompt_steps))
    def _():
        prompt_tile(lambda c: None)

    @pl.when(s == n_prompt_steps)
    def _():
        os_ref[...] = _ffn_tile(xs_ref[...], scs_ref[...], shs_ref[...], gts_ref[...],
                                w_in_bf, w_out_bf, g_ref, b_ref, lambda c: None)


def _ffn(x, xs, mod, mod_idx, w_in, w_out, ln_g, ln_b, *, tile):
    nb, nt, _ = x.shape
    ns = xs.shape[0]
    tiles_per_seq = nt // tile
    n_prompt_steps = nb * tiles_per_seq
    prompt_block = ns // nb

    def tile_index(s):
        p = jnp.minimum(s, n_prompt_steps - 1)
        return (p // tiles_per_seq, p % tiles_per_seq, 0)

    tile_spec = pl.BlockSpec((None, tile, D_MODEL), tile_index)
    rows_spec = pl.BlockSpec((ns, D_MODEL), lambda s: (0, 0))
    prompt_mod = [pl.BlockSpec((None, nb, D_MODEL), functools.partial(lambda s, i: (i, prompt_block, 0), i=i))
                  for i in mod_idx]
    sample_mod = [pl.BlockSpec((None, ns, D_MODEL), functools.partial(lambda s, i: (i, 0, 0), i=i))
                  for i in mod_idx]
    return pl.pallas_call(
        functools.partial(_ffn_kernel, tiles_per_seq, n_prompt_steps),
        grid=(n_prompt_steps + 1,),
        in_specs=[tile_spec, rows_spec] + prompt_mod + sample_mod + [
            pl.BlockSpec(memory_space=pl.ANY),
            pl.BlockSpec(memory_space=pl.ANY),
            _const_spec((1, D_MODEL)),
            _const_spec((1, D_MODEL)),
        ],
        out_specs=[tile_spec, rows_spec],
        out_shape=[jax.ShapeDtypeStruct(x.shape, F32), jax.ShapeDtypeStruct(xs.shape, F32)],
        scratch_shapes=[
            pltpu.VMEM((D_MODEL, 2 * D_FF), BF16),
            pltpu.VMEM((D_FF, D_MODEL), BF16),
            pltpu.VMEM((FFN_STAGE_SLOTS, 2, D_MODEL, FF_CHUNK), F32),
            pltpu.VMEM((FFN_STAGE_SLOTS, FF_CHUNK, D_MODEL), F32),
            pltpu.SemaphoreType.DMA((FFN_STAGE_SLOTS, 3)),
        ],
        compiler_params=pltpu.CompilerParams(dimension_semantics=("arbitrary",), vmem_limit_bytes=VMEM_LIMIT),
        name="ffn",
    )(x, xs, mod, mod, mod, mod, mod, mod, w_in, w_out, ln_g, ln_b)


def _zero_from(v):
    bits = pltpu.bitcast(v, U32)
    return pltpu.bitcast(jnp.right_shift(jnp.right_shift(bits, 16), 16), F32)


def _tied_rows(z, after):
    tie = jnp.concatenate([_zero_from(after[0:SUBLANES, 0:LANES])] * (z.shape[0] // SUBLANES), axis=0)
    return jnp.concatenate([z[:, 0:LANES] + tie, z[:, LANES:]], axis=-1)


def _mixer_tail(x, gate_rows, conv, pooled, ga, gb, cg_ref, cb_ref, w_co_ref, pw_ref, ps_ref,
                w_po_ref, w_o_ref):
    ya_in = _layer_norm(conv, cg_ref[...], cb_ref[...])
    ya_in = (ya_in * jax.nn.sigmoid(ya_in)).astype(BF16)
    merged = jax.nn.sigmoid(ga) * _dot(ya_in, w_co_ref[...])
    pooled = pooled.astype(BF16)
    mixed = jnp.concatenate(
        [_dot(pooled[:, g * POOL_GW:(g + 1) * POOL_GW], pw_ref[g]) for g in range(len(POOL_WINDOWS))],
        axis=-1)
    yb = _dot((mixed * ps_ref[...]).astype(BF16), w_po_ref[...])
    merged = (merged + jax.nn.sigmoid(gb) * yb).astype(BF16)
    return DN_ALPHA * x + gate_rows * _dot(merged, w_o_ref[...])


def _window_sum(hist, work, j, w, tile):
    end = HIST_PAD + tile
    lo = {w: HIST_PAD}
    v = w
    while v > 2:
        lo[v // 2] = (lo[v] - v // 2) // SUBLANES * SUBLANES
        v //= 2
    s = hist[j, lo[2]:end, :] + hist[j, lo[2] - 1:end - 1, :]
    v, slot = 2, 0
    while v < w:
        work[j, slot, lo[v]:end, :] = s
        s = work[j, slot, lo[2 * v]:end, :] + work[j, slot, lo[2 * v] - v:end - v, :]
        v, slot = 2 * v, 1 - slot
    return s


def _pack_rows(rows_f32):
    return pltpu.bitcast(rows_f32.astype(BF16), U32)


def _conv_chunk(conv_even, conv_odd, conv_wb, j, q0):
    loaded = {}

    def words_at(ref, which, offset):
        if (which, offset) not in loaded:
            loaded[which, offset] = ref[j, pl.ds(q0 + offset, SUBLANES), :]
        return loaded[which, offset]

    taps, weights = [], []
    for k in range(CONV_WIDTH):
        back = CONV_HIST - k
        ref, which, first = (conv_even, 0, -(back // 2)) if back % 2 == 0 else (conv_odd, 1, -((back - 1) // 2))
        words = jnp.concatenate(
            [words_at(ref, which, first + i) for i in range(0, CONV_ROWS // 2, SUBLANES)], axis=0)
        taps.append(pltpu.bitcast(words, BF16))
        weights.append(jnp.concatenate([conv_wb[j, k]] * (CONV_ROWS // BF16_ROWS), axis=0))
    taps = jnp.stack(taps, axis=0).astype(F32)
    weights = jnp.stack(weights, axis=0).astype(F32)
    return jnp.sum(taps * weights, axis=0)


def _mixer_kernel(tiles_per_seq, n_tiles,
                  x_ref, sc_ref, sh_ref, gt_ref, w_in_ref, cw_ref, cbias_ref, cg_ref, cb_ref,
                  w_co_ref, pw_ref, ps_ref, w_po_ref, w_o_ref, g_ref, b_ref,
                  o_ref, nconv_ref, npool_ref,
                  conv_hist, conv_even, conv_odd, conv_wb, conv_out, pool_hist, work, z_prev):
    s = pl.program_id(0)

    @pl.when(s == n_tiles)
    def _():
        o_ref[...] = _layer_norm(z_prev[...], g_ref[...], b_ref[...])

    @pl.when(s < n_tiles)
    def _():
        _mixer_tile(s // tiles_per_seq, s % tiles_per_seq, tiles_per_seq, s == 0,
                    x_ref, sc_ref, sh_ref, gt_ref, w_in_ref, cw_ref, cbias_ref, cg_ref, cb_ref,
                    w_co_ref, pw_ref, ps_ref, w_po_ref, w_o_ref, g_ref, b_ref, o_ref, nconv_ref, npool_ref,
                    conv_hist, conv_even, conv_odd, conv_wb, conv_out, pool_hist, work, z_prev)


def _mixer_tile(b, t, tiles_per_seq, first_step,
                x_ref, sc_ref, sh_ref, gt_ref, w_in_ref, cw_ref, cbias_ref, cg_ref, cb_ref,
                w_co_ref, pw_ref, ps_ref, w_po_ref, w_o_ref, g_ref, b_ref, o_ref, nconv_ref, npool_ref,
                conv_hist, conv_even, conv_odd, conv_wb, conv_out, pool_hist, work, z_prev):
    tile = x_ref.shape[0]
    hist_words = HIST_PAD // 2
    tile_words = tile // 2

    @pl.when(first_step)
    def _():
        z_prev[...] = jnp.zeros(z_prev.shape, F32)
        for j in range(N_LANE_TILES):
            for k in range(CONV_WIDTH):
                conv_wb[j, k] = jnp.broadcast_to(cw_ref[k:k + 1, _lane_tile(j)], (BF16_ROWS, LANES)).astype(BF16)

    @pl.when(t == 0)
    def _():
        conv_hist[:, 0:HIST_PAD, :] = jnp.zeros((N_LANE_TILES, HIST_PAD, LANES), F32)
        pool_hist[:, 0:HIST_PAD, :] = jnp.zeros((N_LANE_TILES, HIST_PAD, LANES), F32)
        conv_even[:, 0:hist_words, :] = jnp.zeros((N_LANE_TILES, hist_words, LANES), U32)
        conv_odd[:, 0:hist_words, :] = jnp.zeros((N_LANE_TILES, hist_words, LANES), U32)

    @pl.when(t > 0)
    def _():
        conv_hist[:, 0:HIST_PAD, :] = conv_hist[:, tile:tile + HIST_PAD, :]
        pool_hist[:, 0:HIST_PAD, :] = pool_hist[:, tile:tile + HIST_PAD, :]
        conv_even[:, 0:hist_words, :] = conv_even[:, tile_words:tile_words + hist_words, :]
        conv_odd[:, 0:hist_words, :] = conv_odd[:, tile_words:tile_words + hist_words, :]

    def norm_prev_piece(p, after):
        r0 = p * MIX_LN_ROWS
        z = _tied_rows(z_prev[r0:r0 + MIX_LN_ROWS, :], after)
        o_ref[r0:r0 + MIX_LN_ROWS, :] = _layer_norm(z, g_ref[...], b_ref[...])

    x = x_ref[...]
    h = (x * (1.0 + sc_ref[pl.ds(b, 1), :]) + sh_ref[pl.ds(b, 1), :]).astype(BF16)

    parts = ([], [])
    for c in range(D_MODEL // POOL_GW):
        for half in range(2):
            c0 = half * D_MODEL + c * POOL_GW
            d = _dot(h, w_in_ref[:, c0:c0 + POOL_GW])
            parts[half].append(d)
            norm_prev_piece(2 * c + half, d)
    glu = jnp.concatenate(parts[0], axis=-1) * jax.nn.sigmoid(jnp.concatenate(parts[1], axis=-1))

    for j in range(N_LANE_TILES):
        rows = glu[:, _lane_tile(j)]
        conv_hist[j, HIST_PAD:HIST_PAD + tile, :] = rows
        conv_even[j, hist_words:hist_words + tile_words, :] = _pack_rows(rows)
    for j in range(N_LANE_TILES):
        conv_odd[j, hist_words:hist_words + tile_words, :] = _pack_rows(
            conv_hist[j, HIST_PAD - 1:HIST_PAD - 1 + tile, :])

    u = _dot(h, w_in_ref[:, 2 * D_MODEL:3 * D_MODEL])
    for j in range(N_LANE_TILES):
        pool_hist[j, HIST_PAD:HIST_PAD + tile, :] = u[:, _lane_tile(j)]
    ga = _dot(h, w_in_ref[:, 3 * D_MODEL:4 * D_MODEL])
    gb = _dot(h, w_in_ref[:, 4 * D_MODEL:5 * D_MODEL])

    def conv_rows(c, carry):
        r0 = pl.multiple_of(c * CONV_ROWS, CONV_ROWS)
        q0 = hist_words + pl.multiple_of(c * (CONV_ROWS // 2), CONV_ROWS // 2)
        for j in range(N_LANE_TILES):
            conv_out[j, pl.ds(r0, CONV_ROWS), :] = _conv_chunk(conv_even, conv_odd, conv_wb, j, q0)
        return carry

    lax.fori_loop(0, tile // CONV_ROWS, conv_rows, 0)
    conv = jnp.concatenate([conv_out[j] + cbias_ref[:, _lane_tile(j)] for j in range(N_LANE_TILES)], axis=-1)

    pos1 = t * tile + 1 + lax.broadcasted_iota(jnp.int32, (POOL_MAX, LANES), 0)
    pooled = []
    for j in range(N_LANE_TILES):
        w = POOL_WINDOWS[j * LANES // POOL_GW]
        s = _window_sum(pool_hist, work, j, w, tile)
        head = s[0:POOL_MAX] / jnp.minimum(w, pos1).astype(F32)
        mean = jnp.concatenate([head, s[POOL_MAX:] * (1.0 / w)], axis=0)
        pooled.append(mean - u[:, _lane_tile(j)])
    pooled = jnp.concatenate(pooled, axis=-1)

    z_prev[...] = _mixer_tail(x, gt_ref[pl.ds(b, 1), :], conv, pooled, ga, gb, cg_ref, cb_ref,
                              w_co_ref, pw_ref, ps_ref, w_po_ref, w_o_ref)

    @pl.when(first_step)
    def _():
        nconv_ref[...] = jnp.zeros(nconv_ref.shape, F32)
        npool_ref[...] = jnp.zeros(npool_ref.shape, F32)

    @pl.when(t == tiles_per_seq - 1)
    def _():
        nb = nconv_ref.shape[1]
        mine = lax.broadcasted_iota(jnp.int32, (nb, LANES), 0) == b
        last = HIST_PAD + tile
        for hist, out_ref, n_rows in ((conv_hist, nconv_ref, CONV_HIST), (pool_hist, npool_ref, POOL_HIST)):
            for j in range(N_LANE_TILES):
                for r in range(n_rows):
                    src = last - n_rows + r
                    new = jnp.broadcast_to(hist[j, src:src + 1, :], (nb, LANES))
                    out_ref[r, :, _lane_tile(j)] = jnp.where(mine, new, out_ref[r, :, _lane_tile(j)])


def _mixer_weight_specs():
    return [
        _const_spec((CONV_WIDTH, D_MODEL)),
        _const_spec((1, D_MODEL)),
        _const_spec((1, D_MODEL)),
        _const_spec((1, D_MODEL)),
        _const_spec((D_MODEL, D_MODEL)),
        _const_spec((len(POOL_WINDOWS), POOL_GW, POOL_GW)),
        _const_spec((1, D_MODEL)),
        _const_spec((D_MODEL, D_MODEL)),
        _const_spec((D_MODEL, D_MODEL)),
        _const_spec((1, D_MODEL)),
        _const_spec((1, D_MODEL)),
    ]


def _mixer_prompt(x, mod, w_in, tail_w, *, tile):
    nb, nt, _ = x.shape
    assert tile >= HIST_PAD and tile % CONV_ROWS == 0 and nt % tile == 0
    assert tile % MIX_LN_ROWS == 0 and tile // MIX_LN_ROWS == 2 * (D_MODEL // POOL_GW)
    prompt_block = (mod.shape[1] - nb) // nb
    tiles_per_seq = nt // tile
    n_tiles = nb * tiles_per_seq

    def tile_index(s):
        p = jnp.clip(s, 0, n_tiles - 1)
        return (p // tiles_per_seq, p % tiles_per_seq, 0)

    mod_specs = [pl.BlockSpec((None, nb, D_MODEL), functools.partial(lambda s, i: (i, prompt_block, 0), i=i))
                 for i in (4, 3, 5)]
    hist = (N_LANE_TILES, HIST_PAD + tile, LANES)
    hist_packed = (N_LANE_TILES, (HIST_PAD + tile) // 2, LANES)
    return pl.pallas_call(
        functools.partial(_mixer_kernel, tiles_per_seq, n_tiles),
        grid=(n_tiles + 1,),
        in_specs=[pl.BlockSpec((None, tile, D_MODEL), tile_index)] + mod_specs
        + [_const_spec((D_MODEL, 5 * D_MODEL))] + _mixer_weight_specs(),
        out_specs=[
            pl.BlockSpec((None, tile, D_MODEL), lambda s: tile_index(s - 1)),
            pl.BlockSpec((CONV_HIST, nb, D_MODEL), lambda s: (0, 0, 0)),
            pl.BlockSpec((POOL_HIST, nb, D_MODEL), lambda s: (0, 0, 0)),
        ],
        out_shape=[
            jax.ShapeDtypeStruct(x.shape, F32),
            jax.ShapeDtypeStruct((CONV_HIST, nb, D_MODEL), F32),
            jax.ShapeDtypeStruct((POOL_HIST, nb, D_MODEL), F32),
        ],
        scratch_shapes=[
            pltpu.VMEM(hist, F32),
            pltpu.VMEM(hist_packed, U32),
            pltpu.VMEM(hist_packed, U32),
            pltpu.VMEM((N_LANE_TILES, CONV_WIDTH, BF16_ROWS, LANES), BF16),
            pltpu.VMEM((N_LANE_TILES, tile, LANES), F32),
            pltpu.VMEM(hist, F32),
            pltpu.VMEM((N_LANE_TILES, 2) + hist[1:], F32),
            pltpu.VMEM((tile, D_MODEL), F32),
        ],
        compiler_params=pltpu.CompilerParams(dimension_semantics=("arbitrary",), vmem_limit_bytes=VMEM_LIMIT),
        name="mixer",
    )(x, mod, mod, mod, w_in, *tail_w)


def _sample_proj_kernel(x_ref, sc_ref, sh_ref, w_in_ref, glu_ref, u_ref, ga_ref, gb_ref):
    h = (x_ref[...] * (1.0 + sc_ref[...]) + sh_ref[...]).astype(BF16)
    a = _dot(h, w_in_ref[:, 0:D_MODEL])
    glu_ref[...] = a * jax.nn.sigmoid(_dot(h, w_in_ref[:, D_MODEL:2 * D_MODEL]))
    u_ref[...] = _dot(h, w_in_ref[:, 2 * D_MODEL:3 * D_MODEL])
    ga_ref[...] = _dot(h, w_in_ref[:, 3 * D_MODEL:4 * D_MODEL])
    gb_ref[...] = _dot(h, w_in_ref[:, 4 * D_MODEL:5 * D_MODEL])


def _sample_state_kernel(sconv_ref, spool_ref, glu_ref, u_ref, cw_ref, cbias_ref,
                         nconv_ref, npool_ref, conv_ref, pooled_ref):
    glu = glu_ref[...]
    conv = cbias_ref[...] + glu * cw_ref[CONV_HIST:CONV_WIDTH, :]
    for k in range(CONV_HIST):
        conv = conv + sconv_ref[k] * cw_ref[k:k + 1, :]
    conv_ref[...] = conv
    nconv_ref[0:CONV_HIST - 1] = sconv_ref[1:CONV_HIST]
    nconv_ref[CONV_HIST - 1] = glu

    u = u_ref[...]
    w = jnp.left_shift(2, pl.program_id(0))
    s = u
    for back in range(1, POOL_MAX):
        s = s + jnp.where(back < w, spool_ref[POOL_HIST - back], 0.0)
    pooled_ref[...] = s / jnp.minimum(w, PAST_LEN + 1).astype(F32) - u
    npool_ref[0:POOL_HIST - 1] = spool_ref[1:POOL_HIST]
    npool_ref[POOL_HIST - 1] = u


def _sample_tail_kernel(x_ref, gt_ref, conv_ref, pooled_ref, ga_ref, gb_ref, cg_ref, cb_ref, w_co_ref,
                        pw_ref, ps_ref, w_po_ref, w_o_ref, g_ref, b_ref, o_ref):
    z = _mixer_tail(x_ref[...], gt_ref[...], conv_ref[...], pooled_ref[...], ga_ref[...], gb_ref[...],
                    cg_ref, cb_ref, w_co_ref, pw_ref, ps_ref, w_po_ref, w_o_ref)
    o_ref[...] = _layer_norm(z, g_ref[...], b_ref[...])


def _mixer_sample(x, mod, state_conv, state_pool, w_in, tail_w):
    ns = x.shape[0]
    row_spec = pl.BlockSpec((ns, D_MODEL), lambda i: (0, 0))
    row_shape = jax.ShapeDtypeStruct((ns, D_MODEL), F32)

    def mod_spec(i):
        return pl.BlockSpec((None, ns, D_MODEL), lambda _: (i, 0, 0))

    glu, u, ga, gb = pl.pallas_call(
        _sample_proj_kernel,
        grid=(1,),
        in_specs=[row_spec, mod_spec(4), mod_spec(3), _const_spec((D_MODEL, 5 * D_MODEL))],
        out_specs=[row_spec] * 4,
        out_shape=[row_shape] * 4,
        compiler_params=pltpu.CompilerParams(dimension_semantics=("arbitrary",), vmem_limit_bytes=VMEM_LIMIT),
        name="sample_proj",
    )(x, mod, mod, w_in)

    conv_w, conv_b = tail_w[0], tail_w[1]
    tok_spec = pl.BlockSpec((ns, POOL_GW), lambda g: (0, g))
    nconv, npool, conv, pooled = pl.pallas_call(
        _sample_state_kernel,
        grid=(len(POOL_WINDOWS),),
        in_specs=[
            pl.BlockSpec((CONV_HIST, ns, POOL_GW), lambda g: (0, 0, g)),
            pl.BlockSpec((POOL_HIST, ns, POOL_GW), lambda g: (0, 0, g)),
            tok_spec, tok_spec,
            pl.BlockSpec((CONV_WIDTH, POOL_GW), lambda g: (0, g)),
            pl.BlockSpec((1, POOL_GW), lambda g: (0, g)),
        ],
        out_specs=[
            pl.BlockSpec((CONV_HIST, ns, POOL_GW), lambda g: (0, 0, g)),
            pl.BlockSpec((POOL_HIST, ns, POOL_GW), lambda g: (0, 0, g)),
            tok_spec, tok_spec,
        ],
        out_shape=[
            jax.ShapeDtypeStruct((CONV_HIST, ns, D_MODEL), F32),
            jax.ShapeDtypeStruct((POOL_HIST, ns, D_MODEL), F32),
            row_shape, row_shape,
        ],
        compiler_params=pltpu.CompilerParams(dimension_semantics=("arbitrary",), vmem_limit_bytes=VMEM_LIMIT),
        name="sample_state",
    )(state_conv, state_pool, glu, u, conv_w, conv_b)

    out = pl.pallas_call(
        _sample_tail_kernel,
        grid=(1,),
        in_specs=[row_spec, mod_spec(5), row_spec, row_spec, row_spec, row_spec] + _mixer_weight_specs()[2:],
        out_specs=row_spec,
        out_shape=row_shape,
        compiler_params=pltpu.CompilerParams(dimension_semantics=("arbitrary",), vmem_limit_bytes=VMEM_LIMIT),
        name="sample_tail",
    )(x, mod, conv, pooled, ga, gb, *tail_w[2:])
    return out, nconv, npool


def _time_major(state):
    return jnp.transpose(state, (0, 2, 1, 3))[0]


def _batch_major(state):
    return jnp.transpose(state[None], (0, 2, 1, 3))


def kernel(x_prompt, x_sample, state_conv, state_pool, c_prompt, c_sample, w_ada, b_ada, ffn1_w_in, ffn1_w_out, ln1_g, ln1_b, w_in, conv_w, conv_b, conv_ln_g, conv_ln_b, w_conv_out, pool_w, pool_scale, w_pool_out, w_out, ln2_g, ln2_b, ffn2_w_in, ffn2_w_out, ln3_g, ln3_b):
    nb, nt, d = x_prompt.shape
    ns, st, _ = x_sample.shape
    assert d == D_MODEL and st == 1 and w_ada.shape[0] == DEPTH
    assert nt % TOKEN_TILE == 0 and nt % FFN_TILE == 0 and ns % nb == 0 and N_MOD % ADA_GROUP == 0

    mod = _ada(c_sample, c_prompt, w_ada.reshape(D_MODEL, N_MOD * D_MODEL), b_ada[0])

    row = lambda v: v[0].reshape(1, D_MODEL)
    ffn1 = (ffn1_w_in.reshape(D_MODEL, 2 * D_FF), ffn1_w_out.reshape(D_FF, D_MODEL), row(ln1_g), row(ln1_b))
    ffn2 = (ffn2_w_in.reshape(D_MODEL, 2 * D_FF), ffn2_w_out.reshape(D_FF, D_MODEL), row(ln3_g), row(ln3_b))
    w_in_b = w_in[0].astype(BF16)
    tail_w = (conv_w[0], row(conv_b), row(conv_ln_g), row(conv_ln_b), w_conv_out[0].astype(BF16),
              pool_w[0].astype(BF16), row(pool_scale), w_pool_out[0].astype(BF16), w_out[0].astype(BF16),
              row(ln2_g), row(ln2_b))

    xp, xs = _ffn(x_prompt, x_sample.reshape(ns, D_MODEL), mod, (1, 0, 2), *ffn1, tile=FFN_TILE)
    xp, nconv_p, npool_p = _mixer_prompt(xp, mod, w_in_b, tail_w, tile=TOKEN_TILE)
    xs, nconv_s, npool_s = _mixer_sample(xs, mod, _time_major(state_conv), _time_major(state_pool),
                                         w_in_b, tail_w)
    xp, xs = _ffn(xp, xs, mod, (7, 6, 8), *ffn2, tile=FFN_TILE)

    return (xp, xs.reshape(ns, 1, D_MODEL), _batch_major(nconv_p), _batch_major(npool_p),
            _batch_major(nconv_s), _batch_major(npool_s))
```

```python
import functools

import jax
import jax.numpy as jnp
from jax import lax
from jax.experimental import pallas as pl
from jax.experimental.pallas import tpu as pltpu

D_MODEL = 1024
CONV_WIDTH = 31
CONV_HIST = CONV_WIDTH - 1
POOL_WINDOWS = (2, 4, 8, 16)
POOL_GW = D_MODEL // len(POOL_WINDOWS)
POOL_MAX = 16
POOL_HIST = POOL_MAX - 1
D_FF = ((8 * D_MODEL // 3 + 127) // 128) * 128
N_MOD = 9
ADA_GROUP = 3
DEPTH = 1
DN_ALPHA = (2.0 * DEPTH) ** 0.25
FFN_RES = 0.5
LN_EPS = 1e-5
PAST_LEN = 16384

SUBLANES = 8
LANES = 128
BF16_ROWS = 2 * SUBLANES
N_LANE_TILES = D_MODEL // LANES
FF_CHUNK = 256
N_FF_CHUNKS = D_FF // FF_CHUNK
FFN_STAGE_SLOTS = 2
TOKEN_TILE = 512
FFN_TILE = 512
MIX_LN_ROWS = 64
HIST_PAD = 32
CONV_ROWS = 256
VMEM_LIMIT = 56 * 1024 * 1024

BF16 = jnp.bfloat16
F32 = jnp.float32
U32 = jnp.uint32

assert all(w & (w - 1) == 0 for w in POOL_WINDOWS) and POOL_GW == 2 * LANES
assert HIST_PAD >= CONV_HIST and HIST_PAD % BF16_ROWS == 0


def _const_spec(shape):
    n = len(shape)
    return pl.BlockSpec(shape, lambda *_: (0,) * n, pipeline_mode=pl.Buffered(1))


def _dot(a, b):
    return jnp.dot(a, b, preferred_element_type=F32)


def _layer_norm(z, g, b):
    mu = jnp.mean(z, axis=-1, keepdims=True)
    zc = z - mu
    var = jnp.mean(zc * zc, axis=-1, keepdims=True)
    return zc * lax.rsqrt(var + LN_EPS) * g + b


def _lane_tile(j):
    return slice(j * LANES, (j + 1) * LANES)


def _ada_kernel(cs_ref, cp_ref, w_ref, b_ref, o_ref):
    c = jnp.concatenate([cs_ref[...], cp_ref[...]], axis=0)
    s = (c * jax.nn.sigmoid(c)).astype(BF16)
    for q in range(ADA_GROUP):
        cols = slice(q * D_MODEL, (q + 1) * D_MODEL)
        o_ref[q] = _dot(s, w_ref[:, cols].astype(BF16)) + b_ref[:, cols]


def _ada(c_sample, c_prompt, w_ada, b_ada):
    ns, nb = c_sample.shape[0], c_prompt.shape[0]
    return pl.pallas_call(
        _ada_kernel,
        grid=(N_MOD // ADA_GROUP,),
        in_specs=[
            pl.BlockSpec((ns, D_MODEL), lambda i: (0, 0)),
            pl.BlockSpec((nb, D_MODEL), lambda i: (0, 0)),
            pl.BlockSpec((D_MODEL, ADA_GROUP * D_MODEL), lambda i: (0, i)),
            pl.BlockSpec((1, ADA_GROUP * D_MODEL), lambda i: (0, i)),
        ],
        out_specs=pl.BlockSpec((ADA_GROUP, ns + nb, D_MODEL), lambda i: (i, 0, 0)),
        out_shape=jax.ShapeDtypeStruct((N_MOD, ns + nb, D_MODEL), F32),
        compiler_params=pltpu.CompilerParams(dimension_semantics=("arbitrary",), vmem_limit_bytes=VMEM_LIMIT),
        name="ada",
    )(c_sample, c_prompt, w_ada, b_ada.reshape(1, N_MOD * D_MODEL))


def _ffn_chunk_copies(c, slot, w_in_hbm, w_out_hbm, stage_in, stage_out, sems):
    c0 = c * FF_CHUNK
    return (
        pltpu.make_async_copy(w_in_hbm.at[:, pl.ds(c0, FF_CHUNK)], stage_in.at[slot, 0], sems.at[slot, 0]),
        pltpu.make_async_copy(w_in_hbm.at[:, pl.ds(D_FF + c0, FF_CHUNK)], stage_in.at[slot, 1], sems.at[slot, 1]),
        pltpu.make_async_copy(w_out_hbm.at[pl.ds(c0, FF_CHUNK), :], stage_out.at[slot], sems.at[slot, 2]),
    )


def _ffn_tile(x, sc, sh, gt, w_in_bf, w_out_bf, g_ref, b_ref, prepare_chunk):
    h = (x * (1.0 + sc) + sh).astype(BF16)
    acc = jnp.zeros(x.shape, F32)
    for c in range(N_FF_CHUNKS):
        prepare_chunk(c)
        c0 = c * FF_CHUNK
        gate = _dot(h, w_in_bf[:, c0:c0 + FF_CHUNK])
        up = _dot(h, w_in_bf[:, D_FF + c0:D_FF + c0 + FF_CHUNK])
        act = (gate * jax.nn.sigmoid(gate) * up).astype(BF16)
        acc = acc + _dot(act, w_out_bf[c0:c0 + FF_CHUNK, :])
    z = DN_ALPHA * x + FFN_RES * gt * acc
    return _layer_norm(z, g_ref[...], b_ref[...])


def _ffn_kernel(tiles_per_seq, n_prompt_steps,
                x_ref, xs_ref, sc_ref, sh_ref, gt_ref, scs_ref, shs_ref, gts_ref, w_in_hbm, w_out_hbm, g_ref, b_ref,
                o_ref, os_ref, w_in_bf, w_out_bf, stage_in, stage_out, sems):
    s = pl.program_id(0)
    copies = functools.partial(_ffn_chunk_copies, w_in_hbm=w_in_hbm, w_out_hbm=w_out_hbm,
                               stage_in=stage_in, stage_out=stage_out, sems=sems)

    def load_chunk(c):
        slot = c % FFN_STAGE_SLOTS
        c0 = c * FF_CHUNK
        for copy in copies(c, slot):
            copy.wait()
        w_in_bf[:, c0:c0 + FF_CHUNK] = stage_in[slot, 0].astype(BF16)
        w_in_bf[:, D_FF + c0:D_FF + c0 + FF_CHUNK] = stage_in[slot, 1].astype(BF16)
        w_out_bf[c0:c0 + FF_CHUNK, :] = stage_out[slot].astype(BF16)
        if c + FFN_STAGE_SLOTS < N_FF_CHUNKS:
            for copy in copies(c + FFN_STAGE_SLOTS, slot):
                copy.start()

    def prompt_tile(prepare_chunk):
        seq = s // tiles_per_seq
        o_ref[...] = _ffn_tile(x_ref[...], sc_ref[pl.ds(seq, 1), :], sh_ref[pl.ds(seq, 1), :],
                               gt_ref[pl.ds(seq, 1), :], w_in_bf, w_out_bf, g_ref, b_ref, prepare_chunk)

    @pl.when(s == 0)
    def _():
        for c in range(FFN_STAGE_SLOTS):
            for copy in copies(c, c):
                copy.start()
        prompt_tile(load_chunk)

    @pl.when((s > 0) & (s < n_prompt_steps))
    def _():
        prompt_tile(lambda c: None)

    @pl.when(s == n_prompt_steps)
    def _():
        os_ref[...] = _ffn_tile(xs_ref[...], scs_ref[...], shs_ref[...], gts_ref[...],
                                w_in_bf, w_out_bf, g_ref, b_ref, lambda c: None)


def _ffn(x, xs, mod, mod_idx, w_in, w_out, ln_g, ln_b, *, tile):
    nb, nt, _ = x.shape
    ns = xs.shape[0]
    tiles_per_seq = nt // tile
    n_prompt_steps = nb * tiles_per_seq
    prompt_block = ns // nb

    def tile_index(s):
        p = jnp.minimum(s, n_prompt_steps - 1)
        return (p // tiles_per_seq, p % tiles_per_seq, 0)

    tile_spec = pl.BlockSpec((None, tile, D_MODEL), tile_index)
    rows_spec = pl.BlockSpec((ns, D_MODEL), lambda s: (0, 0))
    prompt_mod = [pl.BlockSpec((None, nb, D_MODEL), functools.partial(lambda s, i: (i, prompt_block, 0), i=i))
                  for i in mod_idx]
    sample_mod = [pl.BlockSpec((None, ns, D_MODEL), functools.partial(lambda s, i: (i, 0, 0), i=i))
                  for i in mod_idx]
    return pl.pallas_call(
        functools.partial(_ffn_kernel, tiles_per_seq, n_prompt_steps),
        grid=(n_prompt_steps + 1,),
        in_specs=[tile_spec, rows_spec] + prompt_mod + sample_mod + [
            pl.BlockSpec(memory_space=pl.ANY),
            pl.BlockSpec(memory_space=pl.ANY),
            _const_spec((1, D_MODEL)),
            _const_spec((1, D_MODEL)),
        ],
        out_specs=[tile_spec, rows_spec],
        out_shape=[jax.ShapeDtypeStruct(x.shape, F32), jax.ShapeDtypeStruct(xs.shape, F32)],
        scratch_shapes=[
            pltpu.VMEM((D_MODEL, 2 * D_FF), BF16),
            pltpu.VMEM((D_FF, D_MODEL), BF16),
            pltpu.VMEM((FFN_STAGE_SLOTS, 2, D_MODEL, FF_CHUNK), F32),
            pltpu.VMEM((FFN_STAGE_SLOTS, FF_CHUNK, D_MODEL), F32),
            pltpu.SemaphoreType.DMA((FFN_STAGE_SLOTS, 3)),
        ],
        compiler_params=pltpu.CompilerParams(dimension_semantics=("arbitrary",), vmem_limit_bytes=VMEM_LIMIT),
        name="ffn",
    )(x, xs, mod, mod, mod, mod, mod, mod, w_in, w_out, ln_g, ln_b)


def _zero_from(v):
    bits = pltpu.bitcast(v, U32)
    return pltpu.bitcast(jnp.right_shift(jnp.right_shift(bits, 16), 16), F32)


def _tied_rows(z, after):
    tie = jnp.concatenate([_zero_from(after[0:SUBLANES, 0:LANES])] * (z.shape[0] // SUBLANES), axis=0)
    return jnp.concatenate([z[:, 0:LANES] + tie, z[:, LANES:]], axis=-1)


def _mixer_tail(x, gate_rows, conv, pooled, ga, gb, cg_ref, cb_ref, w_co_ref, pw_ref, ps_ref,
                w_po_ref, w_o_ref, interleave):
    n_groups = len(POOL_WINDOWS)
    pooled = pooled.astype(BF16)
    mixed = [_dot(pooled[:, g * POOL_GW:(g + 1) * POOL_GW], pw_ref[g]) for g in range(n_groups)]
    scaled = (jnp.concatenate(mixed, axis=-1) * ps_ref[...]).astype(BF16)
    yb = [_dot(scaled, w_po_ref[:, g * POOL_GW:(g + 1) * POOL_GW]) for g in range(n_groups)]
    if interleave:
        anchors = mixed + yb
        rows = conv.shape[0] // len(anchors)
        ya_in = jnp.concatenate(
            [_layer_norm(_tied_rows(conv[p * rows:(p + 1) * rows], anchors[p]), cg_ref[...], cb_ref[...])
             for p in range(len(anchors))], axis=0)
    else:
        ya_in = _layer_norm(conv, cg_ref[...], cb_ref[...])
    yb = jnp.concatenate(yb, axis=-1)
    ya_in = (ya_in * jax.nn.sigmoid(ya_in)).astype(BF16)
    merged = jax.nn.sigmoid(ga) * _dot(ya_in, w_co_ref[...])
    merged = (merged + jax.nn.sigmoid(gb) * yb).astype(BF16)
    return DN_ALPHA * x + gate_rows * _dot(merged, w_o_ref[...])


def _window_sum(hist, work, j, w, tile):
    end = HIST_PAD + tile
    lo = {w: HIST_PAD}
    v = w
    while v > 2:
        lo[v // 2] = (lo[v] - v // 2) // SUBLANES * SUBLANES
        v //= 2
    s = hist[j, lo[2]:end, :] + hist[j, lo[2] - 1:end - 1, :]
    v, slot = 2, 0
    while v < w:
        work[j, slot, lo[v]:end, :] = s
        s = work[j, slot, lo[2 * v]:end, :] + work[j, slot, lo[2 * v] - v:end - v, :]
        v, slot = 2 * v, 1 - slot
    return s


def _pack_rows(rows_f32):
    return pltpu.bitcast(rows_f32.astype(BF16), U32)


def _conv_chunk(conv_even, conv_odd, conv_wb, j, q0):
    loaded = {}

    def words_at(ref, which, offset):
        if (which, offset) not in loaded:
            loaded[which, offset] = ref[j, pl.ds(q0 + offset, SUBLANES), :]
        return loaded[which, offset]

    taps, weights = [], []
    for k in range(CONV_WIDTH):
        back = CONV_HIST - k
        ref, which, first = (conv_even, 0, -(back // 2)) if back % 2 == 0 else (conv_odd, 1, -((back - 1) // 2))
        words = jnp.concatenate(
            [words_at(ref, which, first + i) for i in range(0, CONV_ROWS // 2, SUBLANES)], axis=0)
        taps.append(pltpu.bitcast(words, BF16))
        weights.append(jnp.concatenate([conv_wb[j, k]] * (CONV_ROWS // BF16_ROWS), axis=0))
    taps = jnp.stack(taps, axis=0).astype(F32)
    weights = jnp.stack(weights, axis=0).astype(F32)
    return jnp.sum(taps * weights, axis=0)


def _mixer_kernel(tiles_per_seq, n_tiles,
                  x_ref, sc_ref, sh_ref, gt_ref, w_in_ref, cw_ref, cbias_ref, cg_ref, cb_ref,
                  w_co_ref, pw_ref, ps_ref, w_po_ref, w_o_ref, g_ref, b_ref,
                  o_ref, nconv_ref, npool_ref,
                  conv_hist, conv_even, conv_odd, conv_wb, conv_out, pool_hist, work, z_prev):
    s = pl.program_id(0)

    @pl.when(s == n_tiles)
    def _():
        o_ref[...] = _layer_norm(z_prev[...], g_ref[...], b_ref[...])

    @pl.when(s < n_tiles)
    def _():
        _mixer_tile(s // tiles_per_seq, s % tiles_per_seq, tiles_per_seq, s == 0,
                    x_ref, sc_ref, sh_ref, gt_ref, w_in_ref, cw_ref, cbias_ref, cg_ref, cb_ref,
                    w_co_ref, pw_ref, ps_ref, w_po_ref, w_o_ref, g_ref, b_ref, o_ref, nconv_ref, npool_ref,
                    conv_hist, conv_even, conv_odd, conv_wb, conv_out, pool_hist, work, z_prev)


def _mixer_tile(b, t, tiles_per_seq, first_step,
                x_ref, sc_ref, sh_ref, gt_ref, w_in_ref, cw_ref, cbias_ref, cg_ref, cb_ref,
                w_co_ref, pw_ref, ps_ref, w_po_ref, w_o_ref, g_ref, b_ref, o_ref, nconv_ref, npool_ref,
                conv_hist, conv_even, conv_odd, conv_wb, conv_out, pool_hist, work, z_prev):
    tile = x_ref.shape[0]
    hist_words = HIST_PAD // 2
    tile_words = tile // 2

    @pl.when(first_step)
    def _():
        z_prev[...] = jnp.zeros(z_prev.shape, F32)
        for j in range(N_LANE_TILES):
            for k in range(CONV_WIDTH):
                conv_wb[j, k] = jnp.broadcast_to(cw_ref[k:k + 1, _lane_tile(j)], (BF16_ROWS, LANES)).astype(BF16)

    @pl.when(t == 0)
    def _():
        conv_hist[:, 0:HIST_PAD, :] = jnp.zeros((N_LANE_TILES, HIST_PAD, LANES), F32)
        pool_hist[:, 0:HIST_PAD, :] = jnp.zeros((N_LANE_TILES, HIST_PAD, LANES), F32)
        conv_even[:, 0:hist_words, :] = jnp.zeros((N_LANE_TILES, hist_words, LANES), U32)
        conv_odd[:, 0:hist_words, :] = jnp.zeros((N_LANE_TILES, hist_words, LANES), U32)

    @pl.when(t > 0)
    def _():
        conv_hist[:, 0:HIST_PAD, :] = conv_hist[:, tile:tile + HIST_PAD, :]
        pool_hist[:, 0:HIST_PAD, :] = pool_hist[:, tile:tile + HIST_PAD, :]
        conv_even[:, 0:hist_words, :] = conv_even[:, tile_words:tile_words + hist_words, :]
        conv_odd[:, 0:hist_words, :] = conv_odd[:, tile_words:tile_words + hist_words, :]

    def norm_prev_piece(p, after):
        r0 = p * MIX_LN_ROWS
        z = _tied_rows(z_prev[r0:r0 + MIX_LN_ROWS, :], after)
        o_ref[r0:r0 + MIX_LN_ROWS, :] = _layer_norm(z, g_ref[...], b_ref[...])

    x = x_ref[...]
    h = (x * (1.0 + sc_ref[pl.ds(b, 1), :]) + sh_ref[pl.ds(b, 1), :]).astype(BF16)

    parts = ([], [])
    for c in range(D_MODEL // POOL_GW):
        for half in range(2):
            c0 = half * D_MODEL + c * POOL_GW
            d = _dot(h, w_in_ref[:, c0:c0 + POOL_GW])
            parts[half].append(d)
            norm_prev_piece(2 * c + half, d)
    glu = jnp.concatenate(parts[0], axis=-1) * jax.nn.sigmoid(jnp.concatenate(parts[1], axis=-1))

    for j in range(N_LANE_TILES):
        rows = glu[:, _lane_tile(j)]
        conv_hist[j, HIST_PAD:HIST_PAD + tile, :] = rows
        conv_even[j, hist_words:hist_words + tile_words, :] = _pack_rows(rows)
    for j in range(N_LANE_TILES):
        conv_odd[j, hist_words:hist_words + tile_words, :] = _pack_rows(
            conv_hist[j, HIST_PAD - 1:HIST_PAD - 1 + tile, :])

    u = _dot(h, w_in_ref[:, 2 * D_MODEL:3 * D_MODEL])
    for j in range(N_LANE_TILES):
        pool_hist[j, HIST_PAD:HIST_PAD + tile, :] = u[:, _lane_tile(j)]
    ga = _dot(h, w_in_ref[:, 3 * D_MODEL:4 * D_MODEL])
    gb = _dot(h, w_in_ref[:, 4 * D_MODEL:5 * D_MODEL])

    def conv_rows(c, carry):
        r0 = pl.multiple_of(c * CONV_ROWS, CONV_ROWS)
        q0 = hist_words + pl.multiple_of(c * (CONV_ROWS // 2), CONV_ROWS // 2)
        for j in range(N_LANE_TILES):
            conv_out[j, pl.ds(r0, CONV_ROWS), :] = _conv_chunk(conv_even, conv_odd, conv_wb, j, q0)
        return carry

    lax.fori_loop(0, tile // CONV_ROWS, conv_rows, 0)
    conv = jnp.concatenate([conv_out[j] + cbias_ref[:, _lane_tile(j)] for j in range(N_LANE_TILES)], axis=-1)

    pos1 = t * tile + 1 + lax.broadcasted_iota(jnp.int32, (POOL_MAX, LANES), 0)
    pooled = []
    for j in range(N_LANE_TILES):
        w = POOL_WINDOWS[j * LANES // POOL_GW]
        s = _window_sum(pool_hist, work, j, w, tile)
        head = s[0:POOL_MAX] / jnp.minimum(w, pos1).astype(F32)
        mean = jnp.concatenate([head, s[POOL_MAX:] * (1.0 / w)], axis=0)
        pooled.append(mean - u[:, _lane_tile(j)])
    pooled = jnp.concatenate(pooled, axis=-1)

    z_prev[...] = _mixer_tail(x, gt_ref[pl.ds(b, 1), :], conv, pooled, ga, gb, cg_ref, cb_ref,
                              w_co_ref, pw_ref, ps_ref, w_po_ref, w_o_ref, interleave=True)

    @pl.when(first_step)
    def _():
        nconv_ref[...] = jnp.zeros(nconv_ref.shape, F32)
        npool_ref[...] = jnp.zeros(npool_ref.shape, F32)

    @pl.when(t == tiles_per_seq - 1)
    def _():
        nb = nconv_ref.shape[1]
        mine = lax.broadcasted_iota(jnp.int32, (nb, LANES), 0) == b
        last = HIST_PAD + tile
        for hist, out_ref, n_rows in ((conv_hist, nconv_ref, CONV_HIST), (pool_hist, npool_ref, POOL_HIST)):
            for j in range(N_LANE_TILES):
                for r in range(n_rows):
                    src = last - n_rows + r
                    new = jnp.broadcast_to(hist[j, src:src + 1, :], (nb, LANES))
                    out_ref[r, :, _lane_tile(j)] = jnp.where(mine, new, out_ref[r, :, _lane_tile(j)])


def _mixer_weight_specs():
    return [
        _const_spec((CONV_WIDTH, D_MODEL)),
        _const_spec((1, D_MODEL)),
        _const_spec((1, D_MODEL)),
        _const_spec((1, D_MODEL)),
        _const_spec((D_MODEL, D_MODEL)),
        _const_spec((len(POOL_WINDOWS), POOL_GW, POOL_GW)),
        _const_spec((1, D_MODEL)),
        _const_spec((D_MODEL, D_MODEL)),
        _const_spec((D_MODEL, D_MODEL)),
        _const_spec((1, D_MODEL)),
        _const_spec((1, D_MODEL)),
    ]


def _mixer_prompt(x, mod, w_in, tail_w, *, tile):
    nb, nt, _ = x.shape
    assert tile >= HIST_PAD and tile % CONV_ROWS == 0 and nt % tile == 0
    assert tile % MIX_LN_ROWS == 0 and tile // MIX_LN_ROWS == 2 * (D_MODEL // POOL_GW)
    prompt_block = (mod.shape[1] - nb) // nb
    tiles_per_seq = nt // tile
    n_tiles = nb * tiles_per_seq

    def tile_index(s):
        p = jnp.clip(s, 0, n_tiles - 1)
        return (p // tiles_per_seq, p % tiles_per_seq, 0)

    mod_specs = [pl.BlockSpec((None, nb, D_MODEL), functools.partial(lambda s, i: (i, prompt_block, 0), i=i))
                 for i in (4, 3, 5)]
    hist = (N_LANE_TILES, HIST_PAD + tile, LANES)
    hist_packed = (N_LANE_TILES, (HIST_PAD + tile) // 2, LANES)
    return pl.pallas_call(
        functools.partial(_mixer_kernel, tiles_per_seq, n_tiles),
        grid=(n_tiles + 1,),
        in_specs=[pl.BlockSpec((None, tile, D_MODEL), tile_index)] + mod_specs
        + [_const_spec((D_MODEL, 5 * D_MODEL))] + _mixer_weight_specs(),
        out_specs=[
            pl.BlockSpec((None, tile, D_MODEL), lambda s: tile_index(s - 1)),
            pl.BlockSpec((CONV_HIST, nb, D_MODEL), lambda s: (0, 0, 0)),
            pl.BlockSpec((POOL_HIST, nb, D_MODEL), lambda s: (0, 0, 0)),
        ],
        out_shape=[
            jax.ShapeDtypeStruct(x.shape, F32),
            jax.ShapeDtypeStruct((CONV_HIST, nb, D_MODEL), F32),
            jax.ShapeDtypeStruct((POOL_HIST, nb, D_MODEL), F32),
        ],
        scratch_shapes=[
            pltpu.VMEM(hist, F32),
            pltpu.VMEM(hist_packed, U32),
            pltpu.VMEM(hist_packed, U32),
            pltpu.VMEM((N_LANE_TILES, CONV_WIDTH, BF16_ROWS, LANES), BF16),
            pltpu.VMEM((N_LANE_TILES, tile, LANES), F32),
            pltpu.VMEM(hist, F32),
            pltpu.VMEM((N_LANE_TILES, 2) + hist[1:], F32),
            pltpu.VMEM((tile, D_MODEL), F32),
        ],
        compiler_params=pltpu.CompilerParams(dimension_semantics=("arbitrary",), vmem_limit_bytes=VMEM_LIMIT),
        name="mixer",
    )(x, mod, mod, mod, w_in, *tail_w)


def _sample_proj_kernel(x_ref, sc_ref, sh_ref, w_in_ref, glu_ref, u_ref, ga_ref, gb_ref):
    h = (x_ref[...] * (1.0 + sc_ref[...]) + sh_ref[...]).astype(BF16)
    a = _dot(h, w_in_ref[:, 0:D_MODEL])
    glu_ref[...] = a * jax.nn.sigmoid(_dot(h, w_in_ref[:, D_MODEL:2 * D_MODEL]))
    u_ref[...] = _dot(h, w_in_ref[:, 2 * D_MODEL:3 * D_MODEL])
    ga_ref[...] = _dot(h, w_in_ref[:, 3 * D_MODEL:4 * D_MODEL])
    gb_ref[...] = _dot(h, w_in_ref[:, 4 * D_MODEL:5 * D_MODEL])


def _sample_state_kernel(sconv_ref, spool_ref, glu_ref, u_ref, cw_ref, cbias_ref,
                         nconv_ref, npool_ref, conv_ref, pooled_ref):
    glu = glu_ref[...]
    conv = cbias_ref[...] + glu * cw_ref[CONV_HIST:CONV_WIDTH, :]
    for k in range(CONV_HIST):
        conv = conv + sconv_ref[k] * cw_ref[k:k + 1, :]
    conv_ref[...] = conv
    nconv_ref[0:CONV_HIST - 1] = sconv_ref[1:CONV_HIST]
    nconv_ref[CONV_HIST - 1] = glu

    u = u_ref[...]
    w = jnp.left_shift(2, pl.program_id(0))
    s = u
    for back in range(1, POOL_MAX):
        s = s + jnp.where(back < w, spool_ref[POOL_HIST - back], 0.0)
    pooled_ref[...] = s / jnp.minimum(w, PAST_LEN + 1).astype(F32) - u
    npool_ref[0:POOL_HIST - 1] = spool_ref[1:POOL_HIST]
    npool_ref[POOL_HIST - 1] = u


def _sample_tail_kernel(x_ref, gt_ref, conv_ref, pooled_ref, ga_ref, gb_ref, cg_ref, cb_ref, w_co_ref,
                        pw_ref, ps_ref, w_po_ref, w_o_ref, g_ref, b_ref, o_ref):
    z = _mixer_tail(x_ref[...], gt_ref[...], conv_ref[...], pooled_ref[...], ga_ref[...], gb_ref[...],
                    cg_ref, cb_ref, w_co_ref, pw_ref, ps_ref, w_po_ref, w_o_ref, interleave=False)
    o_ref[...] = _layer_norm(z, g_ref[...], b_ref[...])


def _mixer_sample(x, mod, state_conv, state_pool, w_in, tail_w):
    ns = x.shape[0]
    row_spec = pl.BlockSpec((ns, D_MODEL), lambda i: (0, 0))
    row_shape = jax.ShapeDtypeStruct((ns, D_MODEL), F32)

    def mod_spec(i):
        return pl.BlockSpec((None, ns, D_MODEL), lambda _: (i, 0, 0))

    glu, u, ga, gb = pl.pallas_call(
        _sample_proj_kernel,
        grid=(1,),
        in_specs=[row_spec, mod_spec(4), mod_spec(3), _const_spec((D_MODEL, 5 * D_MODEL))],
        out_specs=[row_spec] * 4,
        out_shape=[row_shape] * 4,
        compiler_params=pltpu.CompilerParams(dimension_semantics=("arbitrary",), vmem_limit_bytes=VMEM_LIMIT),
        name="sample_proj",
    )(x, mod, mod, w_in)

    conv_w, conv_b = tail_w[0], tail_w[1]
    tok_spec = pl.BlockSpec((ns, POOL_GW), lambda g: (0, g))
    nconv, npool, conv, pooled = pl.pallas_call(
        _sample_state_kernel,
        grid=(len(POOL_WINDOWS),),
        in_specs=[
            pl.BlockSpec((CONV_HIST, ns, POOL_GW), lambda g: (0, 0, g)),
            pl.BlockSpec((POOL_HIST, ns, POOL_GW), lambda g: (0, 0, g)),
            tok_spec, tok_spec,
            pl.BlockSpec((CONV_WIDTH, POOL_GW), lambda g: (0, g)),
            pl.BlockSpec((1, POOL_GW), lambda g: (0, g)),
        ],
        out_specs=[
            pl.BlockSpec((CONV_HIST, ns, POOL_GW), lambda g: (0, 0, g)),
            pl.BlockSpec((POOL_HIST, ns, POOL_GW), lambda g: (0, 0, g)),
            tok_spec, tok_spec,
        ],
        out_shape=[
            jax.ShapeDtypeStruct((CONV_HIST, ns, D_MODEL), F32),
            jax.ShapeDtypeStruct((POOL_HIST, ns, D_MODEL), F32),
            row_shape, row_shape,
        ],
        compiler_params=pltpu.CompilerParams(dimension_semantics=("arbitrary",), vmem_limit_bytes=VMEM_LIMIT),
        name="sample_state",
    )(state_conv, state_pool, glu, u, conv_w, conv_b)

    out = pl.pallas_call(
        _sample_tail_kernel,
        grid=(1,),
        in_specs=[row_spec, mod_spec(5), row_spec, row_spec, row_spec, row_spec] + _mixer_weight_specs()[2:],
        out_specs=row_spec,
        out_shape=row_shape,
        compiler_params=pltpu.CompilerParams(dimension_semantics=("arbitrary",), vmem_limit_bytes=VMEM_LIMIT),
        name="sample_tail",
    )(x, mod, conv, pooled, ga, gb, *tail_w[2:])
    return out, nconv, npool


def _time_major(state):
    return jnp.transpose(state, (0, 2, 1, 3))[0]


def _batch_major(state):
    return jnp.transpose(state[None], (0, 2, 1, 3))


def kernel(x_prompt, x_sample, state_conv, state_pool, c_prompt, c_sample, w_ada, b_ada, ffn1_w_in, ffn1_w_out, ln1_g, ln1_b, w_in, conv_w, conv_b, conv_ln_g, conv_ln_b, w_conv_out, pool_w, pool_scale, w_pool_out, w_out, ln2_g, ln2_b, ffn2_w_in, ffn2_w_out, ln3_g, ln3_b):
    nb, nt, d = x_prompt.shape
    ns, st, _ = x_sample.shape
    assert d == D_MODEL and st == 1 and w_ada.shape[0] == DEPTH
    assert nt % TOKEN_TILE == 0 and nt % FFN_TILE == 0 and ns % nb == 0 and N_MOD % ADA_GROUP == 0

    mod = _ada(c_sample, c_prompt, w_ada.reshape(D_MODEL, N_MOD * D_MODEL), b_ada[0])

    row = lambda v: v[0].reshape(1, D_MODEL)
    ffn1 = (ffn1_w_in.reshape(D_MODEL, 2 * D_FF), ffn1_w_out.reshape(D_FF, D_MODEL), row(ln1_g), row(ln1_b))
    ffn2 = (ffn2_w_in.reshape(D_MODEL, 2 * D_FF), ffn2_w_out.reshape(D_FF, D_MODEL), row(ln3_g), row(ln3_b))
    w_in_b = w_in[0].astype(BF16)
    tail_w = (conv_w[0], row(conv_b), row(conv_ln_g), row(conv_ln_b), w_conv_out[0].astype(BF16),
              pool_w[0].astype(BF16), row(pool_scale), w_pool_out[0].astype(BF16), w_out[0].astype(BF16),
              row(ln2_g), row(ln2_b))

    xp, xs = _ffn(x_prompt, x_sample.reshape(ns, D_MODEL), mod, (1, 0, 2), *ffn1, tile=FFN_TILE)
    xp, nconv_p, npool_p = _mixer_prompt(xp, mod, w_in_b, tail_w, tile=TOKEN_TILE)
    xs, nconv_s, npool_s = _mixer_sample(xs, mod, _time_major(state_conv), _time_major(state_pool),
                                         w_in_b, tail_w)
    xp, xs = _ffn(xp, xs, mod, (7, 6, 8), *ffn2, tile=FFN_TILE)

    return (xp, xs.reshape(ns, 1, D_MODEL), _batch_major(nconv_p), _batch_major(npool_p),
            _batch_major(nconv_s), _batch_major(npool_s))
```

```python
import functools

import jax
import jax.numpy as jnp
from jax import lax
from jax.experimental import pallas as pl
from jax.experimental.pallas import tpu as pltpu

D_MODEL = 1024
CONV_WIDTH = 31
CONV_HIST = CONV_WIDTH - 1
POOL_WINDOWS = (2, 4, 8, 16)
POOL_GW = D_MODEL // len(POOL_WINDOWS)
POOL_MAX = 16
POOL_HIST = POOL_MAX - 1
D_FF = ((8 * D_MODEL // 3 + 127) // 128) * 128
N_MOD = 9
ADA_GROUP = 3
DEPTH = 1
DN_ALPHA = (2.0 * DEPTH) ** 0.25
FFN_RES = 0.5
LN_EPS = 1e-5
PAST_LEN = 16384

SUBLANES = 8
LANES = 128
BF16_ROWS = 2 * SUBLANES
N_LANE_TILES = D_MODEL // LANES
FF_CHUNK = 256
N_FF_CHUNKS = D_FF // FF_CHUNK
FFN_STAGE_SLOTS = 2
TOKEN_TILE = 512
FFN_TILE = 512
MIX_LN_ROWS = 64
HIST_PAD = 32
CONV_ROWS = 256
W_IN_CHUNKS = 5 * D_MODEL // POOL_GW
MIX_W_CHUNKS = W_IN_CHUNKS + 3 * (D_MODEL // POOL_GW)
MIX_STAGE_SLOTS = N_LANE_TILES
VMEM_LIMIT = 56 * 1024 * 1024

BF16 = jnp.bfloat16
F32 = jnp.float32
U32 = jnp.uint32

assert all(w & (w - 1) == 0 for w in POOL_WINDOWS) and POOL_GW == 2 * LANES
assert HIST_PAD >= CONV_HIST and HIST_PAD % BF16_ROWS == 0


def _const_spec(shape):
    n = len(shape)
    return pl.BlockSpec(shape, lambda *_: (0,) * n, pipeline_mode=pl.Buffered(1))


def _dot(a, b):
    return jnp.dot(a, b, preferred_element_type=F32)


def _layer_norm(z, g, b):
    mu = jnp.mean(z, axis=-1, keepdims=True)
    zc = z - mu
    var = jnp.mean(zc * zc, axis=-1, keepdims=True)
    return zc * lax.rsqrt(var + LN_EPS) * g + b


def _lane_tile(j):
    return slice(j * LANES, (j + 1) * LANES)


def _ada_kernel(cs_ref, cp_ref, w_ref, b_ref, o_ref):
    c = jnp.concatenate([cs_ref[...], cp_ref[...]], axis=0)
    s = (c * jax.nn.sigmoid(c)).astype(BF16)
    for q in range(ADA_GROUP):
        cols = slice(q * D_MODEL, (q + 1) * D_MODEL)
        o_ref[q] = _dot(s, w_ref[:, cols].astype(BF16)) + b_ref[:, cols]


def _ada(c_sample, c_prompt, w_ada, b_ada):
    ns, nb = c_sample.shape[0], c_prompt.shape[0]
    return pl.pallas_call(
        _ada_kernel,
        grid=(N_MOD // ADA_GROUP,),
        in_specs=[
            pl.BlockSpec((ns, D_MODEL), lambda i: (0, 0)),
            pl.BlockSpec((nb, D_MODEL), lambda i: (0, 0)),
            pl.BlockSpec((D_MODEL, ADA_GROUP * D_MODEL), lambda i: (0, i)),
            pl.BlockSpec((1, ADA_GROUP * D_MODEL), lambda i: (0, i)),
        ],
        out_specs=pl.BlockSpec((ADA_GROUP, ns + nb, D_MODEL), lambda i: (i, 0, 0)),
        out_shape=jax.ShapeDtypeStruct((N_MOD, ns + nb, D_MODEL), F32),
        compiler_params=pltpu.CompilerParams(dimension_semantics=("arbitrary",), vmem_limit_bytes=VMEM_LIMIT),
        name="ada",
    )(c_sample, c_prompt, w_ada, b_ada.reshape(1, N_MOD * D_MODEL))


def _ffn_chunk_copies(c, slot, w_in_hbm, w_out_hbm, stage_in, stage_out, sems):
    c0 = c * FF_CHUNK
    return (
        pltpu.make_async_copy(w_in_hbm.at[:, pl.ds(c0, FF_CHUNK)], stage_in.at[slot, 0], sems.at[slot, 0]),
        pltpu.make_async_copy(w_in_hbm.at[:, pl.ds(D_FF + c0, FF_CHUNK)], stage_in.at[slot, 1], sems.at[slot, 1]),
        pltpu.make_async_copy(w_out_hbm.at[pl.ds(c0, FF_CHUNK), :], stage_out.at[slot], sems.at[slot, 2]),
    )


def _ffn_tile(x, sc, sh, gt, w_in_bf, w_out_bf, g_ref, b_ref, prepare_chunk):
    h = (x * (1.0 + sc) + sh).astype(BF16)
    acc = jnp.zeros(x.shape, F32)
    for c in range(N_FF_CHUNKS):
        prepare_chunk(c)
        c0 = c * FF_CHUNK
        gate = _dot(h, w_in_bf[:, c0:c0 + FF_CHUNK])
        up = _dot(h, w_in_bf[:, D_FF + c0:D_FF + c0 + FF_CHUNK])
        act = (gate * jax.nn.sigmoid(gate) * up).astype(BF16)
        acc = acc + _dot(act, w_out_bf[c0:c0 + FF_CHUNK, :])
    z = DN_ALPHA * x + FFN_RES * gt * acc
    return _layer_norm(z, g_ref[...], b_ref[...])


def _ffn_kernel(tiles_per_seq, n_prompt_steps,
                x_ref, xs_ref, sc_ref, sh_ref, gt_ref, scs_ref, shs_ref, gts_ref, w_in_hbm, w_out_hbm, g_ref, b_ref,
                o_ref, os_ref, w_in_bf, w_out_bf, stage_in, stage_out, sems):
    s = pl.program_id(0)
    copies = functools.partial(_ffn_chunk_copies, w_in_hbm=w_in_hbm, w_out_hbm=w_out_hbm,
                               stage_in=stage_in, stage_out=stage_out, sems=sems)

    def load_chunk(c):
        slot = c % FFN_STAGE_SLOTS
        c0 = c * FF_CHUNK
        for copy in copies(c, slot):
            copy.wait()
        w_in_bf[:, c0:c0 + FF_CHUNK] = stage_in[slot, 0].astype(BF16)
        w_in_bf[:, D_FF + c0:D_FF + c0 + FF_CHUNK] = stage_in[slot, 1].astype(BF16)
        w_out_bf[c0:c0 + FF_CHUNK, :] = stage_out[slot].astype(BF16)
        if c + FFN_STAGE_SLOTS < N_FF_CHUNKS:
            for copy in copies(c + FFN_STAGE_SLOTS, slot):
                copy.start()

    def prompt_tile(prepare_chunk):
        seq = s // tiles_per_seq
        o_ref[...] = _ffn_tile(x_ref[...], sc_ref[pl.ds(seq, 1), :], sh_ref[pl.ds(seq, 1), :],
                               gt_ref[pl.ds(seq, 1), :], w_in_bf, w_out_bf, g_ref, b_ref, prepare_chunk)

    @pl.when(s == 0)
    def _():
        for c in range(FFN_STAGE_SLOTS):
            for copy in copies(c, c):
                copy.start()
        prompt_tile(load_chunk)

    @pl.when((s > 0) & (s < n_prompt_steps))
    def _():
        prompt_tile(lambda c: None)

    @pl.when(s == n_prompt_steps)
    def _():
        os_ref[...] = _ffn_tile(xs_ref[...], scs_ref[...], shs_ref[...], gts_ref[...],
                                w_in_bf, w_out_bf, g_ref, b_ref, lambda c: None)


def _ffn(x, xs, mod, mod_idx, w_in, w_out, ln_g, ln_b, *, tile):
    nb, nt, _ = x.shape
    ns = xs.shape[0]
    tiles_per_seq = nt // tile
    n_prompt_steps = nb * tiles_per_seq
    prompt_block = ns // nb

    def tile_index(s):
        p = jnp.minimum(s, n_prompt_steps - 1)
        return (p // tiles_per_seq, p % tiles_per_seq, 0)

    tile_spec = pl.BlockSpec((None, tile, D_MODEL), tile_index)
    rows_spec = pl.BlockSpec((ns, D_MODEL), lambda s: (0, 0))
    prompt_mod = [pl.BlockSpec((None, nb, D_MODEL), functools.partial(lambda s, i: (i, prompt_block, 0), i=i))
                  for i in mod_idx]
    sample_mod = [pl.BlockSpec((None, ns, D_MODEL), functools.partial(lambda s, i: (i, 0, 0), i=i))
                  for i in mod_idx]
    return pl.pallas_call(
        functools.partial(_ffn_kernel, tiles_per_seq, n_prompt_steps),
        grid=(n_prompt_steps + 1,),
        in_specs=[tile_spec, rows_spec] + prompt_mod + sample_mod + [
            pl.BlockSpec(memory_space=pl.ANY),
            pl.BlockSpec(memory_space=pl.ANY),
            _const_spec((1, D_MODEL)),
            _const_spec((1, D_MODEL)),
        ],
        out_specs=[tile_spec, rows_spec],
        out_shape=[jax.ShapeDtypeStruct(x.shape, F32), jax.ShapeDtypeStruct(xs.shape, F32)],
        scratch_shapes=[
            pltpu.VMEM((D_MODEL, 2 * D_FF), BF16),
            pltpu.VMEM((D_FF, D_MODEL), BF16),
            pltpu.VMEM((FFN_STAGE_SLOTS, 2, D_MODEL, FF_CHUNK), F32),
            pltpu.VMEM((FFN_STAGE_SLOTS, FF_CHUNK, D_MODEL), F32),
            pltpu.SemaphoreType.DMA((FFN_STAGE_SLOTS, 3)),
        ],
        compiler_params=pltpu.CompilerParams(dimension_semantics=("arbitrary",), vmem_limit_bytes=VMEM_LIMIT),
        name="ffn",
    )(x, xs, mod, mod, mod, mod, mod, mod, w_in, w_out, ln_g, ln_b)


def _zero_from(v):
    bits = pltpu.bitcast(v, U32)
    return pltpu.bitcast(jnp.right_shift(jnp.right_shift(bits, 16), 16), F32)


def _tied_rows(z, after):
    tie = jnp.concatenate([_zero_from(after[0:SUBLANES, 0:LANES])] * (z.shape[0] // SUBLANES), axis=0)
    return jnp.concatenate([z[:, 0:LANES] + tie, z[:, LANES:]], axis=-1)


def _mixer_tail(x, gate_rows, conv, pooled, ga, gb, cg_ref, cb_ref, w_co_ref, pw_ref, ps_ref,
                w_po_ref, w_o_ref, interleave, need=lambda n: None):
    n_groups = len(POOL_WINDOWS)
    pooled = pooled.astype(BF16)
    mixed = [_dot(pooled[:, g * POOL_GW:(g + 1) * POOL_GW], pw_ref[g]) for g in range(n_groups)]
    scaled = (jnp.concatenate(mixed, axis=-1) * ps_ref[...]).astype(BF16)
    yb = []
    for g in range(n_groups):
        need(W_IN_CHUNKS + g + 1)
        yb.append(_dot(scaled, w_po_ref[:, g * POOL_GW:(g + 1) * POOL_GW].astype(BF16)))
    if interleave:
        anchors = mixed + yb
        rows = conv.shape[0] // len(anchors)
        ya_in = jnp.concatenate(
            [_layer_norm(_tied_rows(conv[p * rows:(p + 1) * rows], anchors[p]), cg_ref[...], cb_ref[...])
             for p in range(len(anchors))], axis=0)
    else:
        ya_in = _layer_norm(conv, cg_ref[...], cb_ref[...])
    yb = jnp.concatenate(yb, axis=-1)
    ya_in = (ya_in * jax.nn.sigmoid(ya_in)).astype(BF16)
    need(W_IN_CHUNKS + 2 * n_groups)
    merged = jax.nn.sigmoid(ga) * _dot(ya_in, w_co_ref[...].astype(BF16))
    merged = (merged + jax.nn.sigmoid(gb) * yb).astype(BF16)
    need(W_IN_CHUNKS + 3 * n_groups)
    return DN_ALPHA * x + gate_rows * _dot(merged, w_o_ref[...].astype(BF16))


def _window_sum(hist, work, j, w, tile):
    end = HIST_PAD + tile
    lo = {w: HIST_PAD}
    v = w
    while v > 2:
        lo[v // 2] = (lo[v] - v // 2) // SUBLANES * SUBLANES
        v //= 2
    s = hist[j, lo[2]:end, :] + hist[j, lo[2] - 1:end - 1, :]
    v, slot = 2, 0
    while v < w:
        work[j, slot, lo[v]:end, :] = s
        s = work[j, slot, lo[2 * v]:end, :] + work[j, slot, lo[2 * v] - v:end - v, :]
        v, slot = 2 * v, 1 - slot
    return s


def _pack_rows(rows_f32):
    return pltpu.bitcast(rows_f32.astype(BF16), U32)


def _conv_chunk(conv_even, conv_odd, conv_wb, j, q0):
    loaded = {}

    def words_at(ref, which, offset):
        if (which, offset) not in loaded:
            loaded[which, offset] = ref[j, pl.ds(q0 + offset, SUBLANES), :]
        return loaded[which, offset]

    taps, weights = [], []
    for k in range(CONV_WIDTH):
        back = CONV_HIST - k
        ref, which, first = (conv_even, 0, -(back // 2)) if back % 2 == 0 else (conv_odd, 1, -((back - 1) // 2))
        words = jnp.concatenate(
            [words_at(ref, which, first + i) for i in range(0, CONV_ROWS // 2, SUBLANES)], axis=0)
        taps.append(pltpu.bitcast(words, BF16))
        weights.append(jnp.concatenate([conv_wb[j, k]] * (CONV_ROWS // BF16_ROWS), axis=0))
    taps = jnp.stack(taps, axis=0).astype(F32)
    weights = jnp.stack(weights, axis=0).astype(F32)
    return jnp.sum(taps * weights, axis=0)


def _mix_chunk(i, sources, copies):
    n = D_MODEL // POOL_GW
    if i < 2 * n:
        return sources[0], copies[0], (i % 2) * D_MODEL + (i // 2) * POOL_GW
    if i < W_IN_CHUNKS:
        return sources[0], copies[0], i * POOL_GW
    j = i - W_IN_CHUNKS
    return sources[1 + j // n], copies[1 + j // n], (j % n) * POOL_GW


def _mixer_kernel(tiles_per_seq, n_tiles,
                  x_ref, sc_ref, sh_ref, gt_ref, w_in_hbm, cw_ref, cbias_ref, cg_ref, cb_ref,
                  w_co_hbm, pw_ref, ps_ref, w_po_hbm, w_o_hbm, g_ref, b_ref,
                  o_ref, nconv_ref, npool_ref,
                  conv_hist, conv_even, conv_odd, conv_wb, conv_out, pool_hist, work, z_prev,
                  w_in_ref, w_co_ref, w_po_ref, w_o_ref, sems):
    s = pl.program_id(0)
    sources = (w_in_hbm, w_po_hbm, w_co_hbm, w_o_hbm)
    copies = (w_in_ref, w_po_ref, w_co_ref, w_o_ref)

    half = D_MODEL // 2
    n_in = 2 * W_IN_CHUNKS
    n_all = 2 * MIX_W_CHUNKS
    assert work.shape[2] >= half

    def chunk_copies(i):
        src, _, col = _mix_chunk(i // 2, sources, copies)
        col += (i % 2) * LANES
        slot = i % MIX_STAGE_SLOTS
        return [pltpu.make_async_copy(src.at[pl.ds(r * half, half), pl.ds(col, LANES)],
                                      work.at[slot, r, pl.ds(0, half), :], sems.at[slot, r])
                for r in range(2)]

    def start(i):
        for copy in chunk_copies(i):
            copy.start()

    loaded = [0]

    def need(n):
        while loaded[0] < 2 * n:
            i = loaded[0]
            if i == n_in:
                for k in range(MIX_STAGE_SLOTS):
                    start(i + k)
            for copy in chunk_copies(i):
                copy.wait()
            _, dst, col = _mix_chunk(i // 2, sources, copies)
            col += (i % 2) * LANES
            slot = i % MIX_STAGE_SLOTS
            for r in range(2):
                dst[r * half:(r + 1) * half, col:col + LANES] = work[slot, r, 0:half, :].astype(BF16)
            nxt = i + MIX_STAGE_SLOTS
            if nxt < n_all and (nxt < n_in) == (i < n_in):
                start(nxt)
            loaded[0] += 1

    def tile_step(need):
        _mixer_tile(s // tiles_per_seq, s % tiles_per_seq, tiles_per_seq, s == 0,
                    x_ref, sc_ref, sh_ref, gt_ref, w_in_ref, cw_ref, cbias_ref, cg_ref, cb_ref,
                    w_co_ref, pw_ref, ps_ref, w_po_ref, w_o_ref, g_ref, b_ref, o_ref, nconv_ref, npool_ref,
                    conv_hist, conv_even, conv_odd, conv_wb, conv_out, pool_hist, work, z_prev, need)

    @pl.when(s == n_tiles)
    def _():
        o_ref[...] = _layer_norm(z_prev[...], g_ref[...], b_ref[...])

    @pl.when(s == 0)
    def _():
        for i in range(MIX_STAGE_SLOTS):
            start(i)
        tile_step(need)
        assert loaded[0] == n_all

    @pl.when((s > 0) & (s < n_tiles))
    def _():
        tile_step(lambda n: None)


def _mixer_tile(b, t, tiles_per_seq, first_step,
                x_ref, sc_ref, sh_ref, gt_ref, w_in_ref, cw_ref, cbias_ref, cg_ref, cb_ref,
                w_co_ref, pw_ref, ps_ref, w_po_ref, w_o_ref, g_ref, b_ref, o_ref, nconv_ref, npool_ref,
                conv_hist, conv_even, conv_odd, conv_wb, conv_out, pool_hist, work, z_prev, need):
    tile = x_ref.shape[0]
    hist_words = HIST_PAD // 2
    tile_words = tile // 2

    @pl.when(first_step)
    def _():
        z_prev[...] = jnp.zeros(z_prev.shape, F32)
        for j in range(N_LANE_TILES):
            for k in range(CONV_WIDTH):
                conv_wb[j, k] = jnp.broadcast_to(cw_ref[k:k + 1, _lane_tile(j)], (BF16_ROWS, LANES)).astype(BF16)

    @pl.when(t == 0)
    def _():
        conv_hist[:, 0:HIST_PAD, :] = jnp.zeros((N_LANE_TILES, HIST_PAD, LANES), F32)
        pool_hist[:, 0:HIST_PAD, :] = jnp.zeros((N_LANE_TILES, HIST_PAD, LANES), F32)
        conv_even[:, 0:hist_words, :] = jnp.zeros((N_LANE_TILES, hist_words, LANES), U32)
        conv_odd[:, 0:hist_words, :] = jnp.zeros((N_LANE_TILES, hist_words, LANES), U32)

    @pl.when(t > 0)
    def _():
        conv_hist[:, 0:HIST_PAD, :] = conv_hist[:, tile:tile + HIST_PAD, :]
        pool_hist[:, 0:HIST_PAD, :] = pool_hist[:, tile:tile + HIST_PAD, :]
        conv_even[:, 0:hist_words, :] = conv_even[:, tile_words:tile_words + hist_words, :]
        conv_odd[:, 0:hist_words, :] = conv_odd[:, tile_words:tile_words + hist_words, :]

    def norm_prev_piece(p, after):
        r0 = p * MIX_LN_ROWS
        z = _tied_rows(z_prev[r0:r0 + MIX_LN_ROWS, :], after)
        o_ref[r0:r0 + MIX_LN_ROWS, :] = _layer_norm(z, g_ref[...], b_ref[...])

    x = x_ref[...]
    h = (x * (1.0 + sc_ref[pl.ds(b, 1), :]) + sh_ref[pl.ds(b, 1), :]).astype(BF16)

    parts = ([], [])
    for c in range(D_MODEL // POOL_GW):
        for half in range(2):
            c0 = half * D_MODEL + c * POOL_GW
            need(2 * c + half + 1)
            d = _dot(h, w_in_ref[:, c0:c0 + POOL_GW])
            parts[half].append(d)
            norm_prev_piece(2 * c + half, d)
    glu = jnp.concatenate(parts[0], axis=-1) * jax.nn.sigmoid(jnp.concatenate(parts[1], axis=-1))

    for j in range(N_LANE_TILES):
        rows = glu[:, _lane_tile(j)]
        conv_hist[j, HIST_PAD:HIST_PAD + tile, :] = rows
        conv_even[j, hist_words:hist_words + tile_words, :] = _pack_rows(rows)
    for j in range(N_LANE_TILES):
        conv_odd[j, hist_words:hist_words + tile_words, :] = _pack_rows(
            conv_hist[j, HIST_PAD - 1:HIST_PAD - 1 + tile, :])

    per_proj = D_MODEL // POOL_GW
    need(3 * per_proj)
    u = _dot(h, w_in_ref[:, 2 * D_MODEL:3 * D_MODEL])
    for j in range(N_LANE_TILES):
        pool_hist[j, HIST_PAD:HIST_PAD + tile, :] = u[:, _lane_tile(j)]
    need(4 * per_proj)
    ga = _dot(h, w_in_ref[:, 3 * D_MODEL:4 * D_MODEL])
    need(5 * per_proj)
    gb = _dot(h, w_in_ref[:, 4 * D_MODEL:5 * D_MODEL])

    def conv_rows(c, carry):
        r0 = pl.multiple_of(c * CONV_ROWS, CONV_ROWS)
        q0 = hist_words + pl.multiple_of(c * (CONV_ROWS // 2), CONV_ROWS // 2)
        for j in range(N_LANE_TILES):
            conv_out[j, pl.ds(r0, CONV_ROWS), :] = _conv_chunk(conv_even, conv_odd, conv_wb, j, q0)
        return carry

    lax.fori_loop(0, tile // CONV_ROWS, conv_rows, 0)
    conv = jnp.concatenate([conv_out[j] + cbias_ref[:, _lane_tile(j)] for j in range(N_LANE_TILES)], axis=-1)

    pos1 = t * tile + 1 + lax.broadcasted_iota(jnp.int32, (POOL_MAX, LANES), 0)
    pooled = []
    for j in range(N_LANE_TILES):
        w = POOL_WINDOWS[j * LANES // POOL_GW]
        s = _window_sum(pool_hist, work, j, w, tile)
        head = s[0:POOL_MAX] / jnp.minimum(w, pos1).astype(F32)
        mean = jnp.concatenate([head, s[POOL_MAX:] * (1.0 / w)], axis=0)
        pooled.append(mean - u[:, _lane_tile(j)])
    pooled = jnp.concatenate(pooled, axis=-1)

    z_prev[...] = _mixer_tail(x, gt_ref[pl.ds(b, 1), :], conv, pooled, ga, gb, cg_ref, cb_ref,
                              w_co_ref, pw_ref, ps_ref, w_po_ref, w_o_ref, interleave=True, need=need)

    @pl.when(first_step)
    def _():
        nconv_ref[...] = jnp.zeros(nconv_ref.shape, F32)
        npool_ref[...] = jnp.zeros(npool_ref.shape, F32)

    @pl.when(t == tiles_per_seq - 1)
    def _():
        nb = nconv_ref.shape[1]
        mine = lax.broadcasted_iota(jnp.int32, (nb, LANES), 0) == b
        last = HIST_PAD + tile
        for hist, out_ref, n_rows in ((conv_hist, nconv_ref, CONV_HIST), (pool_hist, npool_ref, POOL_HIST)):
            for j in range(N_LANE_TILES):
                for r in range(n_rows):
                    src = last - n_rows + r
                    new = jnp.broadcast_to(hist[j, src:src + 1, :], (nb, LANES))
                    out_ref[r, :, _lane_tile(j)] = jnp.where(mine, new, out_ref[r, :, _lane_tile(j)])


def _mixer_weight_specs(streamed):
    big = (lambda shape: pl.BlockSpec(memory_space=pl.ANY)) if streamed else _const_spec
    return [
        _const_spec((CONV_WIDTH, D_MODEL)),
        _const_spec((1, D_MODEL)),
        _const_spec((1, D_MODEL)),
        _const_spec((1, D_MODEL)),
        big((D_MODEL, D_MODEL)),
        _const_spec((len(POOL_WINDOWS), POOL_GW, POOL_GW)),
        _const_spec((1, D_MODEL)),
        big((D_MODEL, D_MODEL)),
        big((D_MODEL, D_MODEL)),
        _const_spec((1, D_MODEL)),
        _const_spec((1, D_MODEL)),
    ]


def _mixer_prompt(x, mod, w_in, tail_w, *, tile):
    nb, nt, _ = x.shape
    assert tile >= HIST_PAD and tile % CONV_ROWS == 0 and nt % tile == 0
    assert tile % MIX_LN_ROWS == 0 and tile // MIX_LN_ROWS == 2 * (D_MODEL // POOL_GW)
    prompt_block = (mod.shape[1] - nb) // nb
    tiles_per_seq = nt // tile
    n_tiles = nb * tiles_per_seq

    def tile_index(s):
        p = jnp.clip(s, 0, n_tiles - 1)
        return (p // tiles_per_seq, p % tiles_per_seq, 0)

    mod_specs = [pl.BlockSpec((None, nb, D_MODEL), functools.partial(lambda s, i: (i, prompt_block, 0), i=i))
                 for i in (4, 3, 5)]
    hist = (N_LANE_TILES, HIST_PAD + tile, LANES)
    hist_packed = (N_LANE_TILES, (HIST_PAD + tile) // 2, LANES)
    return pl.pallas_call(
        functools.partial(_mixer_kernel, tiles_per_seq, n_tiles),
        grid=(n_tiles + 1,),
        in_specs=[pl.BlockSpec((None, tile, D_MODEL), tile_index)] + mod_specs
        + [pl.BlockSpec(memory_space=pl.ANY)] + _mixer_weight_specs(streamed=True),
        out_specs=[
            pl.BlockSpec((None, tile, D_MODEL), lambda s: tile_index(s - 1)),
            pl.BlockSpec((CONV_HIST, nb, D_MODEL), lambda s: (0, 0, 0)),
            pl.BlockSpec((POOL_HIST, nb, D_MODEL), lambda s: (0, 0, 0)),
        ],
        out_shape=[
            jax.ShapeDtypeStruct(x.shape, F32),
            jax.ShapeDtypeStruct((CONV_HIST, nb, D_MODEL), F32),
            jax.ShapeDtypeStruct((POOL_HIST, nb, D_MODEL), F32),
        ],
        scratch_shapes=[
            pltpu.VMEM(hist, F32),
            pltpu.VMEM(hist_packed, U32),
            pltpu.VMEM(hist_packed, U32),
            pltpu.VMEM((N_LANE_TILES, CONV_WIDTH, BF16_ROWS, LANES), BF16),
            pltpu.VMEM((N_LANE_TILES, tile, LANES), F32),
            pltpu.VMEM(hist, F32),
            pltpu.VMEM((N_LANE_TILES, 2) + hist[1:], F32),
            pltpu.VMEM((tile, D_MODEL), F32),
            pltpu.VMEM((D_MODEL, 5 * D_MODEL), BF16),
            pltpu.VMEM((D_MODEL, D_MODEL), BF16),
            pltpu.VMEM((D_MODEL, D_MODEL), BF16),
            pltpu.VMEM((D_MODEL, D_MODEL), BF16),
            pltpu.SemaphoreType.DMA((MIX_STAGE_SLOTS, 2)),
        ],
        compiler_params=pltpu.CompilerParams(dimension_semantics=("arbitrary",), vmem_limit_bytes=VMEM_LIMIT),
        name="mixer",
    )(x, mod, mod, mod, w_in, *tail_w)


def _sample_proj_kernel(x_ref, sc_ref, sh_ref, w_in_ref, glu_ref, u_ref, ga_ref, gb_ref):
    h = (x_ref[...] * (1.0 + sc_ref[...]) + sh_ref[...]).astype(BF16)
    proj = lambda i: _dot(h, w_in_ref[:, i * D_MODEL:(i + 1) * D_MODEL].astype(BF16))
    glu_ref[...] = proj(0) * jax.nn.sigmoid(proj(1))
    u_ref[...] = proj(2)
    ga_ref[...] = proj(3)
    gb_ref[...] = proj(4)


def _sample_state_kernel(sconv_ref, spool_ref, glu_ref, u_ref, cw_ref, cbias_ref,
                         nconv_ref, npool_ref, conv_ref, pooled_ref):
    glu = glu_ref[...]
    conv = cbias_ref[...] + glu * cw_ref[CONV_HIST:CONV_WIDTH, :]
    for k in range(CONV_HIST):
        conv = conv + sconv_ref[k] * cw_ref[k:k + 1, :]
    conv_ref[...] = conv
    nconv_ref[0:CONV_HIST - 1] = sconv_ref[1:CONV_HIST]
    nconv_ref[CONV_HIST - 1] = glu

    u = u_ref[...]
    w = jnp.left_shift(2, pl.program_id(0))
    s = u
    for back in range(1, POOL_MAX):
        s = s + jnp.where(back < w, spool_ref[POOL_HIST - back], 0.0)
    pooled_ref[...] = s / jnp.minimum(w, PAST_LEN + 1).astype(F32) - u
    npool_ref[0:POOL_HIST - 1] = spool_ref[1:POOL_HIST]
    npool_ref[POOL_HIST - 1] = u


def _sample_tail_kernel(x_ref, gt_ref, conv_ref, pooled_ref, ga_ref, gb_ref, cg_ref, cb_ref, w_co_ref,
                        pw_ref, ps_ref, w_po_ref, w_o_ref, g_ref, b_ref, o_ref):
    z = _mixer_tail(x_ref[...], gt_ref[...], conv_ref[...], pooled_ref[...], ga_ref[...], gb_ref[...],
                    cg_ref, cb_ref, w_co_ref, pw_ref, ps_ref, w_po_ref, w_o_ref, interleave=False)
    o_ref[...] = _layer_norm(z, g_ref[...], b_ref[...])


def _mixer_sample(x, mod, state_conv, state_pool, w_in, tail_w):
    ns = x.shape[0]
    row_spec = pl.BlockSpec((ns, D_MODEL), lambda i: (0, 0))
    row_shape = jax.ShapeDtypeStruct((ns, D_MODEL), F32)

    def mod_spec(i):
        return pl.BlockSpec((None, ns, D_MODEL), lambda _: (i, 0, 0))

    glu, u, ga, gb = pl.pallas_call(
        _sample_proj_kernel,
        grid=(1,),
        in_specs=[row_spec, mod_spec(4), mod_spec(3), _const_spec((D_MODEL, 5 * D_MODEL))],
        out_specs=[row_spec] * 4,
        out_shape=[row_shape] * 4,
        compiler_params=pltpu.CompilerParams(dimension_semantics=("arbitrary",), vmem_limit_bytes=VMEM_LIMIT),
        name="sample_proj",
    )(x, mod, mod, w_in)

    conv_w, conv_b = tail_w[0], tail_w[1]
    tok_spec = pl.BlockSpec((ns, POOL_GW), lambda g: (0, g))
    nconv, npool, conv, pooled = pl.pallas_call(
        _sample_state_kernel,
        grid=(len(POOL_WINDOWS),),
        in_specs=[
            pl.BlockSpec((CONV_HIST, ns, POOL_GW), lambda g: (0, 0, g)),
            pl.BlockSpec((POOL_HIST, ns, POOL_GW), lambda g: (0, 0, g)),
            tok_spec, tok_spec,
            pl.BlockSpec((CONV_WIDTH, POOL_GW), lambda g: (0, g)),
            pl.BlockSpec((1, POOL_GW), lambda g: (0, g)),
        ],
        out_specs=[
            pl.BlockSpec((CONV_HIST, ns, POOL_GW), lambda g: (0, 0, g)),
            pl.BlockSpec((POOL_HIST, ns, POOL_GW), lambda g: (0, 0, g)),
            tok_spec, tok_spec,
        ],
        out_shape=[
            jax.ShapeDtypeStruct((CONV_HIST, ns, D_MODEL), F32),
            jax.ShapeDtypeStruct((POOL_HIST, ns, D_MODEL), F32),
            row_shape, row_shape,
        ],
        compiler_params=pltpu.CompilerParams(dimension_semantics=("arbitrary",), vmem_limit_bytes=VMEM_LIMIT),
        name="sample_state",
    )(state_conv, state_pool, glu, u, conv_w, conv_b)

    out = pl.pallas_call(
        _sample_tail_kernel,
        grid=(1,),
        in_specs=[row_spec, mod_spec(5), row_spec, row_spec, row_spec, row_spec]
        + _mixer_weight_specs(streamed=False)[2:],
        out_specs=row_spec,
        out_shape=row_shape,
        compiler_params=pltpu.CompilerParams(dimension_semantics=("arbitrary",), vmem_limit_bytes=VMEM_LIMIT),
        name="sample_tail",
    )(x, mod, conv, pooled, ga, gb, *tail_w[2:])
    return out, nconv, npool


def _time_major(state):
    return jnp.transpose(state, (0, 2, 1, 3))[0]


def _batch_major(state):
    return jnp.transpose(state[None], (0, 2, 1, 3))


def kernel(x_prompt, x_sample, state_conv, state_pool, c_prompt, c_sample, w_ada, b_ada, ffn1_w_in, ffn1_w_out, ln1_g, ln1_b, w_in, conv_w, conv_b, conv_ln_g, conv_ln_b, w_conv_out, pool_w, pool_scale, w_pool_out, w_out, ln2_g, ln2_b, ffn2_w_in, ffn2_w_out, ln3_g, ln3_b):
    nb, nt, d = x_prompt.shape
    ns, st, _ = x_sample.shape
    assert d == D_MODEL and st == 1 and w_ada.shape[0] == DEPTH
    assert nt % TOKEN_TILE == 0 and nt % FFN_TILE == 0 and ns % nb == 0 and N_MOD % ADA_GROUP == 0

    mod = _ada(c_sample, c_prompt, w_ada.reshape(D_MODEL, N_MOD * D_MODEL), b_ada[0])

    row = lambda v: v[0].reshape(1, D_MODEL)
    ffn1 = (ffn1_w_in.reshape(D_MODEL, 2 * D_FF), ffn1_w_out.reshape(D_FF, D_MODEL), row(ln1_g), row(ln1_b))
    ffn2 = (ffn2_w_in.reshape(D_MODEL, 2 * D_FF), ffn2_w_out.reshape(D_FF, D_MODEL), row(ln3_g), row(ln3_b))
    w_in_b = w_in.reshape(D_MODEL, 5 * D_MODEL)
    big = lambda w: w.reshape(D_MODEL, D_MODEL)
    tail_w = (conv_w[0], row(conv_b), row(conv_ln_g), row(conv_ln_b), big(w_conv_out),
              pool_w[0].astype(BF16), row(pool_scale), big(w_pool_out), big(w_out),
              row(ln2_g), row(ln2_b))

    xp, xs = _ffn(x_prompt, x_sample.reshape(ns, D_MODEL), mod, (1, 0, 2), *ffn1, tile=FFN_TILE)
    xp, nconv_p, npool_p = _mixer_prompt(xp, mod, w_in_b, tail_w, tile=TOKEN_TILE)
    xs, nconv_s, npool_s = _mixer_sample(xs, mod, _time_major(state_conv), _time_major(state_pool),
                                         w_in_b, tail_w)
    xp, xs = _ffn(xp, xs, mod, (7, 6, 8), *ffn2, tile=FFN_TILE)

    return (xp, xs.reshape(ns, 1, D_MODEL), _batch_major(nconv_p), _batch_major(npool_p),
            _batch_major(nconv_s), _batch_major(npool_s))
```
